```python
import math
import jax, jax.numpy as jnp
from jax import lax
import numpy as np

D_MODEL = 1024
BATCH = 2
SEQ = 8192
DEPTH = 1
DEC_BATCH = 16
DEC_SEQ = 32
PAST_LEN = 2048

CHUNK = 64
N_META = 16
ROPE_THETA = 10000.0
EPS = 1e-6
D_FF = 2816
A_HEADS = 8
A_DIM = 64
B_HEADS = 8
B_DIM = 128
IDX_HEADS = 8
IDX_DIM = 64
TOPK_MAX = 256
QBLOCK = 128
A_WIDTH = A_HEADS * 2 * A_DIM
B_WIDTH = B_HEADS * B_DIM
SPLIT_SIZES = (A_WIDTH, A_WIDTH, A_WIDTH, B_WIDTH, B_WIDTH, B_WIDTH,
               IDX_HEADS * IDX_DIM, IDX_DIM, IDX_HEADS, 2 * D_MODEL)
SPLIT_POINTS = tuple(int(s) for s in np.cumsum(SPLIT_SIZES)[:-1])
N_IN_COLS = int(sum(SPLIT_SIZES))

kernel_name = "gated_diffattn_dsa_macaron_stream"


def rms_norm(x, g):
    xf = x.astype(jnp.float32)
    y = xf * lax.rsqrt(jnp.mean(xf * xf, axis=-1, keepdims=True) + EPS)
    return (y * g.astype(jnp.float32)).astype(x.dtype)


def swiglu(x, w_gate, w_up, w_down):
    return (jax.nn.silu(x @ w_gate) * (x @ w_up)) @ w_down


def rope(x, pos):
    half = x.shape[-1] // 2
    inv_freq = ROPE_THETA ** (-jnp.arange(half, dtype=jnp.float32) / half)
    ang = pos.astype(jnp.float32)[:, None] * inv_freq[None, :]
    cos = jnp.cos(ang)[None, :, None, :]
    sin = jnp.sin(ang)[None, :, None, :]
    xf = x.astype(jnp.float32)
    x1, x2 = xf[..., :half], xf[..., half:]
    return jnp.concatenate([x1 * cos - x2 * sin, x2 * cos + x1 * sin], axis=-1).astype(x.dtype)


def mixer_inputs(h, w_in, pos):
    b, n = h.shape[:2]
    aq, ak, av, bq, bk, bv, iq, ik, iw, gates = jnp.split(h @ w_in, SPLIT_POINTS, axis=-1)
    aq = rope(aq.reshape(b, n, 2 * A_HEADS, A_DIM), pos).reshape(b, n, A_HEADS, 2 * A_DIM)
    ak = rope(ak.reshape(b, n, 2 * A_HEADS, A_DIM), pos).reshape(b, n, A_HEADS, 2 * A_DIM)
    av = av.reshape(b, n, A_HEADS, 2 * A_DIM)
    bq = rope(bq.reshape(b, n, B_HEADS, B_DIM), pos)
    bk = rope(bk.reshape(b, n, B_HEADS, B_DIM), pos)
    bv = bv.reshape(b, n, B_HEADS, B_DIM)
    iq = rope(iq.reshape(b, n, IDX_HEADS, IDX_DIM), pos)
    ik = rope(ik.reshape(b, n, 1, IDX_DIM), pos).reshape(b, n, IDX_DIM)
    return aq, ak, av, bq, bk, bv, iq, ik, iw, gates


def map_query_blocks(fn, q_args, q_chunk):
    n = q_chunk.shape[0]
    qb = min(QBLOCK, n)
    n_blk = -(-n // qb)
    pad = n_blk * qb - n

    def to_blocks(a):
        a = jnp.pad(a, [(0, 0), (0, pad)] + [(0, 0)] * (a.ndim - 2))
        return jnp.moveaxis(a.reshape((a.shape[0], n_blk, qb) + a.shape[2:]), 1, 0)

    qc = jnp.pad(q_chunk, (0, pad), constant_values=-1).reshape(n_blk, qb)
    out = lax.map(lambda blk: fn(*blk[0], blk[1]), (tuple(to_blocks(a) for a in q_args), qc))
    out = jnp.moveaxis(out, 0, 1)
    return out.reshape((out.shape[0], n_blk * qb) + out.shape[3:])[:, :n]


def diff_attn_block(q, qc, k, v, kc, lam):
    b, nq = q.shape[:2]
    nk = k.shape[1]
    q = q.reshape(b, nq, A_HEADS, 2, A_DIM)
    k = k.reshape(b, nk, A_HEADS, 2, A_DIM)
    s = jnp.einsum('bqhmd,bkhmd->bmhqk', q, k).astype(jnp.float32) * (A_DIM ** -0.5)
    visible = kc[None, :] <= qc[:, None]
    p = jax.nn.softmax(jnp.where(visible, s, -jnp.inf), axis=-1)
    w = p[:, 0] - lam * p[:, 1]
    return jnp.einsum('bhqk,bkhe->bqhe', w.astype(v.dtype), v)


def dsa_block(q, iq, iw, qc, k, v, ik, kc, n_top):
    rel = jax.nn.relu(jnp.einsum('bqhd,bkd->bqhk', iq, ik).astype(jnp.float32) * (IDX_DIM ** -0.5))
    score = jnp.einsum('bqhk,bqh->bqk', rel, iw.astype(jnp.float32)) * (IDX_HEADS ** -0.5)
    visible = kc[None, :] <= qc[:, None]
    score = jnp.where(visible[None], score, -jnp.inf)
    top_val, top_idx = lax.top_k(score, n_top)
    valid = top_val > -jnp.inf
    k_sel = jax.vmap(lambda kb, ib: kb[ib])(k, top_idx)
    v_sel = jax.vmap(lambda vb, ib: vb[ib])(v, top_idx)
    s = jnp.einsum('bqhd,bqthd->bhqt', q, k_sel).astype(jnp.float32) * (B_DIM ** -0.5)
    s = jnp.where(valid[:, None], s, -jnp.inf)
    p = jax.nn.softmax(s, axis=-1)
    return jnp.einsum('bhqt,bqthd->bqhd', p.astype(v.dtype), v_sel)


def trunk_layer(x, pos, q_chunk, k_chunk, past, n_top, lam, lam_init,
                g_ffn1, w1_gate, w1_up, w1_down, g_mix, w_in, a_subln, w_a, w_b, w_o,
                g_ffn2, w2_gate, w2_up, w2_down):
    b, n = x.shape[:2]
    x = x + 0.5 * swiglu(rms_norm(x, g_ffn1), w1_gate, w1_up, w1_down)
    h = rms_norm(x, g_mix)
    aq, ak, av, bq, bk, bv, iq, ik, iw, gates = mixer_inputs(h, w_in, pos)
    new_rows = (ak, av, bk, bv, ik)
    if past is None:
        keys = new_rows
    else:
        keys = tuple(jnp.concatenate([c, r.astype(c.dtype)], axis=1) for c, r in zip(past, new_rows))
    ak_all, av_all, bk_all, bv_all, ik_all = keys
    oa = map_query_blocks(lambda q, qc: diff_attn_block(q, qc, ak_all, av_all, k_chunk, lam),
                          (aq,), q_chunk)
    oa = rms_norm(oa, a_subln) * (1.0 - lam_init)
    ob = map_query_blocks(lambda q, qi, qw, qc: dsa_block(q, qi, qw, qc, bk_all, bv_all, ik_all, k_chunk, n_top),
                          (bq, iq, iw), q_chunk)
    g_a, g_b = jnp.split(gates, 2, axis=-1)
    merged = (jax.nn.sigmoid(g_a) * (oa.reshape(b, n, A_WIDTH) @ w_a)
              + jax.nn.sigmoid(g_b) * (ob.reshape(b, n, B_WIDTH) @ w_b))
    x = x + merged @ w_o
    x = x + 0.5 * swiglu(rms_norm(x, g_ffn2), w2_gate, w2_up, w2_down)
    return x, new_rows


def setup_inputs(seed: int = 0) -> dict:
    key = jax.random.key(seed)
    ks = jax.random.split(key, 32)

    def nrm(k, shape, scale):
        return jax.random.normal(k, shape, jnp.float32) * scale

    n_cache = N_META + PAST_LEN
    return {
        'x_prompt': nrm(ks[0], (BATCH, SEQ, D_MODEL), 1.0),
        'x_sample': nrm(ks[1], (DEC_BATCH, DEC_SEQ, D_MODEL), 1.0),
        'cache_a_k': nrm(ks[2], (DEPTH, DEC_BATCH, n_cache, A_HEADS, 2 * A_DIM), 1.0),
        'cache_a_v': nrm(ks[3], (DEPTH, DEC_BATCH, n_cache, A_HEADS, 2 * A_DIM), 1.0),
        'cache_b_k': nrm(ks[4], (DEPTH, DEC_BATCH, n_cache, B_HEADS, B_DIM), 1.0),
        'cache_b_v': nrm(ks[5], (DEPTH, DEC_BATCH, n_cache, B_HEADS, B_DIM), 1.0),
        'cache_b_kidx': nrm(ks[6], (DEPTH, DEC_BATCH, n_cache, IDX_DIM), 1.0),
        'meta': nrm(ks[7], (N_META, D_MODEL), 1.0),
        'g_ffn1': 1.0 + nrm(ks[8], (DEPTH, D_MODEL), 0.01),
        'w1_gate': nrm(ks[9], (DEPTH, D_MODEL, D_FF), D_MODEL ** -0.5),
        'w1_up': nrm(ks[10], (DEPTH, D_MODEL, D_FF), D_MODEL ** -0.5),
        'w1_down': nrm(ks[11], (DEPTH, D_FF, D_MODEL), D_FF ** -0.5),
        'g_mix': 1.0 + nrm(ks[12], (DEPTH, D_MODEL), 0.01),
        'w_in': nrm(ks[13], (DEPTH, D_MODEL, N_IN_COLS), D_MODEL ** -0.5),
        'lam_q1': nrm(ks[14], (DEPTH, A_DIM), 0.1),
        'lam_k1': nrm(ks[15], (DEPTH, A_DIM), 0.1),
        'lam_q2': nrm(ks[16], (DEPTH, A_DIM), 0.1),
        'lam_k2': nrm(ks[17], (DEPTH, A_DIM), 0.1),
        'a_subln': 1.0 + nrm(ks[18], (DEPTH, 2 * A_DIM), 0.01),
        'w_a': nrm(ks[19], (DEPTH, A_WIDTH, D_MODEL), A_WIDTH ** -0.5),
        'w_b': nrm(ks[20], (DEPTH, B_WIDTH, D_MODEL), B_WIDTH ** -0.5),
        'w_o': nrm(ks[21], (DEPTH, D_MODEL, D_MODEL), D_MODEL ** -0.5),
        'g_ffn2': 1.0 + nrm(ks[22], (DEPTH, D_MODEL), 0.01),
        'w2_gate': nrm(ks[23], (DEPTH, D_MODEL, D_FF), D_MODEL ** -0.5),
        'w2_up': nrm(ks[24], (DEPTH, D_MODEL, D_FF), D_MODEL ** -0.5),
        'w2_down': nrm(ks[25], (DEPTH, D_FF, D_MODEL), D_FF ** -0.5),
        'g_final': 1.0 + nrm(ks[26], (D_MODEL,), 0.01),
    }


def reference(x_prompt, x_sample, cache_a_k, cache_a_v, cache_b_k, cache_b_v, cache_b_kidx,
              meta, g_ffn1, w1_gate, w1_up, w1_down, g_mix, w_in, lam_q1, lam_k1, lam_q2, lam_k2,
              a_subln, w_a, w_b, w_o, g_ffn2, w2_gate, w2_up, w2_down, g_final):
    n_top_p = min(TOPK_MAX, SEQ // 4)
    n_top_s = min(TOPK_MAX, (PAST_LEN + DEC_SEQ) // 4)
    meta_chunk = jnp.full((N_META,), -1, jnp.int32)
    pos_p = jnp.arange(N_META + SEQ, dtype=jnp.int32)
    chunk_p = jnp.concatenate([meta_chunk, jnp.arange(SEQ, dtype=jnp.int32) // CHUNK])
    new_frames = PAST_LEN + jnp.arange(DEC_SEQ, dtype=jnp.int32)
    pos_s = N_META + new_frames
    chunk_q_s = new_frames // CHUNK
    chunk_k_s = jnp.concatenate([meta_chunk, jnp.arange(PAST_LEN, dtype=jnp.int32) // CHUNK, chunk_q_s])

    xp = jnp.concatenate([jnp.broadcast_to(meta[None].astype(x_prompt.dtype),
                                           (x_prompt.shape[0], N_META, D_MODEL)), x_prompt], axis=1)
    xs = x_sample
    new_p = [[], [], [], [], []]
    new_s = [[], [], [], [], []]
    for l in range(DEPTH):
        lam_init = 0.8 - 0.6 * math.exp(-0.3 * l)
        lam = (jnp.exp(jnp.sum(lam_q1[l].astype(jnp.float32) * lam_k1[l].astype(jnp.float32)))
               - jnp.exp(jnp.sum(lam_q2[l].astype(jnp.float32) * lam_k2[l].astype(jnp.float32)))
               + lam_init)
        weights = (g_ffn1[l], w1_gate[l], w1_up[l], w1_down[l], g_mix[l], w_in[l], a_subln[l],
                   w_a[l], w_b[l], w_o[l], g_ffn2[l], w2_gate[l], w2_up[l], w2_down[l])
        xp, rows_p = trunk_layer(xp, pos_p, chunk_p, chunk_p, None, n_top_p, lam, lam_init, *weights)
        past = (cache_a_k[l], cache_a_v[l], cache_b_k[l], cache_b_v[l], cache_b_kidx[l])
        xs, rows_s = trunk_layer(xs, pos_s, chunk_q_s, chunk_k_s, past, n_top_s, lam, lam_init, *weights)
        for i in range(5):
            new_p[i].append(rows_p[i])
            new_s[i].append(rows_s[i])
    y_prompt = rms_norm(xp, g_final)[:, N_META:]
    y_sample = rms_norm(xs, g_final)
    a_k_p = jnp.stack(new_p[0])
    a_v_p = jnp.stack(new_p[1])
    b_k_p = jnp.stack(new_p[2])
    b_v_p = jnp.stack(new_p[3])
    b_kidx_p = jnp.stack(new_p[4])
    a_k_s = jnp.stack(new_s[0])
    a_v_s = jnp.stack(new_s[1])
    b_k_s = jnp.stack(new_s[2])
    b_v_s = jnp.stack(new_s[3])
    b_kidx_s = jnp.stack(new_s[4])
    return (y_prompt, y_sample, a_k_p, a_v_p, b_k_p, b_v_p, b_kidx_p, a_k_s, a_v_s, b_k_s, b_v_s, b_kidx_s)
```

```python
import functools

import jax
import jax.numpy as jnp
from jax import lax
from jax.experimental import pallas as pl
from jax.experimental.pallas import tpu as pltpu

D_MODEL = 1024
CHUNK = 64
N_META = 16
ROPE_THETA = 10000.0
EPS = 1e-6
A_HEADS = 8
A_DIM = 64
B_HEADS = 8
B_DIM = 128
IDX_HEADS = 8
IDX_DIM = 64
TOPK_MAX = 256
LAM_INIT = 0.2

LANES = 128
N_HEADS = 8
A_WIDTH = A_HEADS * 2 * A_DIM
B_WIDTH = B_HEADS * B_DIM
IQ_WIDTH = IDX_HEADS * IDX_DIM
META_PAD = LANES

VMEM_LIMIT = 48 * 1024 * 1024
NEG = -1e30
INT_MIN = -(2 ** 31)
KEY_NEG_INF = -2139095041

F32 = jnp.float32
BF16 = jnp.bfloat16

_NT = (((1,), (1,)), ((), ()))


def _params():
    return pltpu.CompilerParams(vmem_limit_bytes=VMEM_LIMIT)


def _dot(a, b):
    return jnp.dot(a, b, preferred_element_type=F32)


def _dot_nt(a, b):
    return lax.dot_general(a, b, _NT, preferred_element_type=F32)


def _rms(x, g):
    ms = jnp.mean(x * x, axis=-1, keepdims=True)
    return x * lax.rsqrt(ms + EPS) * g


def _ffn_body(x_ref, g_ref, wg_ref, wu_ref, wd_ref, gn_ref, *rest, n_ff, emit_h):
    if emit_h:
        o_ref, hn_ref, h_scr, acc_scr = rest
    else:
        o_ref, h_scr, acc_scr = rest
    j = pl.program_id(1)

    @pl.when(j == 0)
    def _():
        h_scr[...] = _rms(x_ref[...], g_ref[...]).astype(BF16)
        acc_scr[...] = jnp.zeros_like(acc_scr)

    h = h_scr[...]
    gate = _dot(h, wg_ref[...])
    up = _dot(h, wu_ref[...])
    act = gate * jax.nn.sigmoid(gate) * up
    acc_scr[...] += _dot(act.astype(BF16), wd_ref[...])

    @pl.when(j == n_ff - 1)
    def _():
        y = x_ref[...] + 0.5 * acc_scr[...]
        if emit_h:
            o_ref[...] = y
            hn_ref[...] = _rms(y, gn_ref[...]).astype(BF16)
        else:
            o_ref[...] = _rms(y, gn_ref[...])


def _ffn(x, g, wg, wu, wd, gn, *, tm, emit_h):
    n, d = x.shape
    dff = wg.shape[1]
    tf = 256 if dff % 256 == 0 else LANES
    n_ff = dff // tf
    row = lambda i, j: (i, 0)
    const = lambda i, j: (0, 0)
    out_shape = [jax.ShapeDtypeStruct((n, d), F32)]
    out_specs = [pl.BlockSpec((tm, d), row)]
    if emit_h:
        out_shape.append(jax.ShapeDtypeStruct((n, d), BF16))
        out_specs.append(pl.BlockSpec((tm, d), row))
    return pl.pallas_call(
        functools.partial(_ffn_body, n_ff=n_ff, emit_h=emit_h),
        grid=(n // tm, n_ff),
        in_specs=[
            pl.BlockSpec((tm, d), row),
            pl.BlockSpec((1, d), const),
            pl.BlockSpec((d, tf), lambda i, j: (0, j)),
            pl.BlockSpec((d, tf), lambda i, j: (0, j)),
            pl.BlockSpec((tf, d), lambda i, j: (j, 0)),
            pl.BlockSpec((1, d), const),
        ],
        out_specs=out_specs,
        out_shape=out_shape,
        scratch_shapes=[pltpu.VMEM((tm, d), BF16), pltpu.VMEM((tm, d), F32)],
        compiler_params=_params(),
        name="ffn_h" if emit_h else "ffn_final",
    )(x, g, wg, wu, wd, gn)


def _rope_block(z, tab_ref, kind, t0):
    cos = tab_ref[:, t0 * LANES:(t0 + 1) * LANES]
    s_a = tab_ref[:, (t0 + 1) * LANES:(t0 + 2) * LANES]
    if kind == "d128":
        return z * cos + pltpu.roll(z, 64, 1) * s_a
    s_b = tab_ref[:, (t0 + 2) * LANES:(t0 + 3) * LANES]
    return z * cos + pltpu.roll(z, 96, 1) * s_a + pltpu.roll(z, 32, 1) * s_b


def _proj_body(h_ref, w_ref, tab_ref, *out_refs, segs):
    h = h_ref[...]
    for out_idx, col0, width, rope in segs:
        cw = 256 if width % 256 == 0 else LANES
        for c in range(0, width, cw):
            z = _dot(h, w_ref[:, col0 + c:col0 + c + cw])
            for b in range(cw // LANES):
                zb = z[:, b * LANES:(b + 1) * LANES]
                if rope is not None:
                    zb = _rope_block(zb, tab_ref, rope[0], rope[1])
                out_refs[out_idx][:, c + b * LANES:c + (b + 1) * LANES] = zb


def _proj(h, w, tab, segs, *, tm, name):
    n, d = h.shape
    widths = [s[2] for s in segs]
    row = lambda i: (i, 0)
    return pl.pallas_call(
        functools.partial(_proj_body, segs=tuple(segs)),
        grid=(n // tm,),
        in_specs=[
            pl.BlockSpec((tm, d), row),
            pl.BlockSpec(w.shape, lambda i: (0, 0)),
            pl.BlockSpec((tm, tab.shape[1]), row),
        ],
        out_specs=[pl.BlockSpec((tm, wd), row) for wd in widths],
        out_shape=[jax.ShapeDtypeStruct((n, wd), F32) for wd in widths],
        compiler_params=_params(),
        name=name,
    )(h, w, tab)


def _rope_tables(pos):
    pos = pos.astype(F32)
    inv32 = ROPE_THETA ** (-jnp.arange(32, dtype=F32) / 32)
    inv64 = ROPE_THETA ** (-jnp.arange(64, dtype=F32) / 64)
    a32 = pos[:, None] * inv32[None, :]
    a64 = pos[:, None] * inv64[None, :]
    c32, s32 = jnp.cos(a32), jnp.sin(a32)
    c64, s64 = jnp.cos(a64), jnp.sin(a64)
    z32 = jnp.zeros_like(c32)
    z64 = jnp.zeros_like(c64)
    cat = lambda *xs: jnp.concatenate(xs, axis=1)
    t64 = cat(cat(c32, c32, c32, c32), cat(-s32, z32, -s32, z32), cat(z32, s32, z32, s32))
    t128 = cat(cat(c64, c64), cat(-s64, s64))
    tmisc = cat(cat(c32, c32, jnp.ones_like(c64)), cat(-s32, z32, z64), cat(z32, s32, z64))
    return t64, t128, tmisc


def _project_all(h, pos, w_a, w_b, w_ig, *, tm):
    t64, t128, tmisc = _rope_tables(pos)
    aq, ak, av = _proj(h, w_a, t64,
                       [(0, 0, A_WIDTH, ("d64", 0)), (1, A_WIDTH, A_WIDTH, ("d64", 0)),
                        (2, 2 * A_WIDTH, A_WIDTH, None)], tm=tm, name="proj_a")
    bq, bk, bv = _proj(h, w_b, t128,
                       [(0, 0, B_WIDTH, ("d128", 0)), (1, B_WIDTH, B_WIDTH, ("d128", 0)),
                        (2, 2 * B_WIDTH, B_WIDTH, None)], tm=tm, name="proj_b")
    iq, misc, gates = _proj(h, w_ig, jnp.concatenate([t64, tmisc], axis=1),
                            [(0, 0, IQ_WIDTH, ("d64", 0)), (1, IQ_WIDTH, LANES, ("d64", 3)),
                             (2, IQ_WIDTH + LANES, 2 * D_MODEL, None)], tm=tm, name="proj_ig")
    return aq, ak, av, bq, bk, bv, iq, misc, gates


def _lam(lam_ref):
    lp = lam_ref[...]
    s1 = jnp.sum(lp[0:1] * lp[1:2], axis=-1, keepdims=True)
    s2 = jnp.sum(lp[2:3] * lp[3:4], axis=-1, keepdims=True)
    return jnp.exp(s1) - jnp.exp(s2) + LAM_INIT


def _stack_diff_queries(q):
    lane = lax.broadcasted_iota(jnp.int32, q.shape, 1)
    q = q * (A_DIM ** -0.5)
    q1 = jnp.where(lane < A_DIM, q, 0.0)
    q2 = jnp.where(lane >= A_DIM, q, 0.0)
    return jnp.concatenate([q1, q2], axis=0).astype(BF16)


def _diff_finish(acc, l, n, lam, g):
    o1 = acc[:n] * (1.0 / l[:n])
    o2 = acc[n:] * (1.0 / l[n:])
    o = o1 - lam * o2
    return _rms(o, g) * (1.0 - LAM_INIT)


def _diff_prompt_body(lam_ref, q_ref, k_ref, v_ref, km_ref, vm_ref, g_ref, o_ref, *, tq):
    qi = pl.program_id(2)
    qs = _stack_diff_queries(q_ref[0])

    s = _dot_nt(qs, km_ref[...].astype(BF16))
    col = lax.broadcasted_iota(jnp.int32, s.shape, 1)
    s = jnp.where(col < N_META, s, NEG)
    m = jnp.max(s, axis=-1, keepdims=True)
    p = jnp.exp(s - m)
    l = jnp.sum(p, axis=-1, keepdims=True)
    acc = _dot(p.astype(BF16), vm_ref[...].astype(BF16))

    def step(kj, carry, diagonal):
        m, l, acc = carry
        start = pl.multiple_of(kj * tq, tq)
        k = k_ref[0, pl.ds(start, tq), :].astype(BF16)
        v = v_ref[0, pl.ds(start, tq), :].astype(BF16)
        s = _dot_nt(qs, k)
        if diagonal:
            r = lax.broadcasted_iota(jnp.int32, s.shape, 0)
            c = lax.broadcasted_iota(jnp.int32, s.shape, 1)
            r = jnp.where(r >= tq, r - tq, r)
            s = jnp.where(c // CHUNK <= r // CHUNK, s, NEG)
        m_new = jnp.maximum(m, jnp.max(s, axis=-1, keepdims=True))
        alpha = jnp.exp(m - m_new)
        p = jnp.exp(s - m_new)
        l = alpha * l + jnp.sum(p, axis=-1, keepdims=True)
        acc = alpha * acc + _dot(p.astype(BF16), v)
        return m_new, l, acc

    carry = lax.fori_loop(0, qi, lambda kj, c: step(kj, c, False), (m, l, acc))
    m, l, acc = step(qi, carry, True)
    o_ref[0] = _diff_finish(acc, l, tq, _lam(lam_ref), g_ref[...])


def _diff_prompt(aq, ak, av, akm, avm, lam4, g, *, tq):
    bsz, t, _ = aq.shape
    head_q = lambda b, h, i: (b, i, h)
    head_all = lambda b, h, i: (b, 0, h)
    head_meta = lambda b, h, i: (0, h)
    const = lambda b, h, i: (0, 0)
    return pl.pallas_call(
        functools.partial(_diff_prompt_body, tq=tq),
        grid=(bsz, N_HEADS, t // tq),
        in_specs=[
            pl.BlockSpec(lam4.shape, const),
            pl.BlockSpec((1, tq, LANES), head_q),
            pl.BlockSpec((1, t, LANES), head_all),
            pl.BlockSpec((1, t, LANES), head_all),
            pl.BlockSpec((META_PAD, LANES), head_meta),
            pl.BlockSpec((META_PAD, LANES), head_meta),
            pl.BlockSpec((1, LANES), const),
        ],
        out_specs=pl.BlockSpec((1, tq, LANES), head_q),
        out_shape=jax.ShapeDtypeStruct(aq.shape, F32),
        compiler_params=_params(),
        name="diff_attn_prompt",
    )(lam4, aq, ak, av, akm, avm, g)


def _small_keys(cache_ref, new_ref, sl, s_new):
    meta = cache_ref[0, 0:N_META, sl]
    new = new_ref[:, sl]
    pad = jnp.zeros((META_PAD - N_META - s_new, meta.shape[1]), F32)
    return jnp.concatenate([meta, new, pad], axis=0)


def _diff_sample_body(lam_ref, q_ref, kc_ref, vc_ref, kn_ref, vn_ref, g_ref, o_ref, *, s_new, past):
    lam = _lam(lam_ref)
    for h in range(N_HEADS):
        sl = slice(h * LANES, (h + 1) * LANES)
        qs = _stack_diff_queries(q_ref[:, sl])
        kp = kc_ref[0, N_META:N_META + past, sl].astype(BF16)
        vp = vc_ref[0, N_META:N_META + past, sl].astype(BF16)
        ks = _small_keys(kc_ref, kn_ref, sl, s_new).astype(BF16)
        vs = _small_keys(vc_ref, vn_ref, sl, s_new).astype(BF16)
        s_p = _dot_nt(qs, kp)
        s_s = _dot_nt(qs, ks)
        col = lax.broadcasted_iota(jnp.int32, s_s.shape, 1)
        s_s = jnp.where(col < N_META + s_new, s_s, NEG)
        m = jnp.maximum(jnp.max(s_p, axis=-1, keepdims=True), jnp.max(s_s, axis=-1, keepdims=True))
        p_p = jnp.exp(s_p - m)
        p_s = jnp.exp(s_s - m)
        l = jnp.sum(p_p, axis=-1, keepdims=True) + jnp.sum(p_s, axis=-1, keepdims=True)
        acc = _dot(p_p.astype(BF16), vp) + _dot(p_s.astype(BF16), vs)
        o_ref[:, sl] = _diff_finish(acc, l, s_new, lam, g_ref[...])


def _diff_sample(aq, ak, av, cache_k, cache_v, lam4, g, *, s_new):
    bsz, n_cache, width = cache_k.shape
    past = n_cache - N_META
    rows = lambda b: (b, 0)
    cache = lambda b: (b, 0, 0)
    const = lambda b: (0, 0)
    return pl.pallas_call(
        functools.partial(_diff_sample_body, s_new=s_new, past=past),
        grid=(bsz,),
        in_specs=[
            pl.BlockSpec(lam4.shape, const),
            pl.BlockSpec((s_new, width), rows),
            pl.BlockSpec((1, n_cache, width), cache),
            pl.BlockSpec((1, n_cache, width), cache),
            pl.BlockSpec((s_new, width), rows),
            pl.BlockSpec((s_new, width), rows),
            pl.BlockSpec((1, LANES), const),
        ],
        out_specs=pl.BlockSpec((s_new, width), rows),
        out_shape=jax.ShapeDtypeStruct(aq.shape, F32),
        compiler_params=_params(),
        name="diff_attn_sample",
    )(lam4, aq, cache_k, cache_v, ak, av, g)


def _stack_index_queries(iq, width):
    n = iq.shape[0]
    lane = lax.broadcasted_iota(jnp.int32, (n, LANES), 1)
    parts = []
    for h in range(IDX_HEADS):
        blk = iq[:, (h // 2) * LANES:(h // 2 + 1) * LANES]
        if h % 2:
            blk = pltpu.roll(blk, IDX_DIM, 1)
        blk = jnp.where(lane < IDX_DIM, blk, 0.0) * (IDX_DIM ** -0.5)
        parts.append(blk[:, :width])
    return jnp.concatenate(parts, axis=0).astype(BF16)


def _index_scores(qs, w, keys, n):
    rel = jnp.maximum(_dot_nt(qs, keys.astype(BF16)), 0.0)
    sc = rel[0:n] * w[:, IDX_DIM:IDX_DIM + 1]
    for h in range(1, IDX_HEADS):
        sc = sc + rel[h * n:(h + 1) * n] * w[:, IDX_DIM + h:IDX_DIM + h + 1]
    return sc * (IDX_HEADS ** -0.5)


def _sort_key(x):
    bits = lax.bitcast_convert_type(x, jnp.int32)
    key = bits ^ ((bits >> 31) & 0x7FFFFFFF)
    return jnp.where(key == -1, 0, key)


def _ind(pred):
    return jnp.where(pred, 1.0, 0.0)


def _count(ind):
    return jnp.sum(ind, axis=-1, keepdims=True)


def _select_threshold(count_fn, n_rows, n_top, idx_bits):
    def bit_step(i, thr):
        cand = thr ^ jnp.left_shift(jnp.int32(1), 31 - i)
        cnt = count_fn(lambda key, col: _ind(key >= cand))
        return jnp.where(cnt >= n_top, cand, thr)

    thr = lax.fori_loop(0, 32, bit_step, jnp.full((n_rows, 1), INT_MIN, jnp.int32))
    need = n_top - count_fn(lambda key, col: _ind(key > thr))
    excess = count_fn(lambda key, col: _ind(key == thr)) - need

    def tie_search(_):
        def idx_step(i, cut):
            cand = cut | jnp.left_shift(jnp.int32(1), idx_bits - 1 - i)
            cnt = count_fn(lambda key, col: jnp.where(key == thr, _ind(col < cand), 0.0))
            return jnp.where(cnt < need, cand, cut)
        return lax.fori_loop(0, idx_bits, idx_step, jnp.zeros((n_rows, 1), jnp.int32))

    no_ties = lambda _: jnp.full((n_rows, 1), 2 ** 30, jnp.int32)
    cut = lax.cond(jnp.max(excess) > 0.0, tie_search, no_ties, None)
    return thr, cut


def _selected(key, col, thr, cut):
    tie = jnp.where(key == thr, jnp.where(col <= cut, 1.0, 0.0), 0.0)
    sel = jnp.where(key > thr, 1.0, tie)
    return jnp.where(key > KEY_NEG_INF, sel, 0.0)


def _dsa_index_body(iq_ref, wq_ref, ik_ref, ikm_ref, om_ref, of_ref, key_m, key_f, *, tq, tk, n_tiles_all,
                    n_top, idx_bits):
    qi = pl.program_id(1)
    qs = _stack_index_queries(iq_ref[0], LANES)
    w = wq_ref[0]

    sc = _index_scores(qs, w, ikm_ref[...], tq)
    lane = lax.broadcasted_iota(jnp.int32, (tq, LANES), 1)
    key_m[...] = _sort_key(jnp.where(lane < N_META, sc, -jnp.inf))

    row0 = qi * tq
    n_tiles = (row0 + tq - 1) // tk + 1
    rowc = (row0 + lax.broadcasted_iota(jnp.int32, (tq, tk), 0)) // CHUNK
    colt = lax.broadcasted_iota(jnp.int32, (tq, tk), 1)

    def score_tile(j, _):
        start = pl.multiple_of(j * tk, tk)
        sc = _index_scores(qs, w, ik_ref[0, pl.ds(start, tk), :], tq)
        visible = (start + colt) // CHUNK <= rowc
        key_f[j] = _sort_key(jnp.where(visible, sc, -jnp.inf))
        return 0

    lax.fori_loop(0, n_tiles, score_tile, 0)

    def count_fn(f):
        c = _count(f(key_m[...], lane))
        return lax.fori_loop(
            0, n_tiles, lambda j, c: c + _count(f(key_f[j], META_PAD + j * tk + colt)), c)

    thr, cut = _select_threshold(count_fn, tq, n_top, idx_bits)
    om_ref[0] = _selected(key_m[...], lane, thr, cut).astype(BF16)

    def write_tile(j, _):
        of_ref[0, j] = _selected(key_f[j], META_PAD + j * tk + colt, thr, cut).astype(BF16)
        return 0

    lax.fori_loop(0, n_tiles, write_tile, 0)

    def zero_tile(j, _):
        of_ref[0, j] = jnp.zeros((tq, tk), BF16)
        return 0

    lax.fori_loop(n_tiles, n_tiles_all, zero_tile, 0)


def _dsa_index(iq, misc, misc_meta, *, tq, tk, n_top):
    bsz, t, _ = iq.shape
    nk = t // tk
    idx_bits = (META_PAD + t - 1).bit_length()
    return pl.pallas_call(
        functools.partial(_dsa_index_body, tq=tq, tk=tk, n_tiles_all=nk, n_top=n_top, idx_bits=idx_bits),
        grid=(bsz, t // tq),
        in_specs=[
            pl.BlockSpec((1, tq, IQ_WIDTH), lambda b, i: (b, i, 0)),
            pl.BlockSpec((1, tq, LANES), lambda b, i: (b, i, 0)),
            pl.BlockSpec((1, t, LANES), lambda b, i: (b, 0, 0)),
            pl.BlockSpec((META_PAD, LANES), lambda b, i: (0, 0)),
        ],
        out_specs=[
            pl.BlockSpec((1, tq, LANES), lambda b, i: (b, i, 0)),
            pl.BlockSpec((1, nk, tq, tk), lambda b, i: (b, 0, i, 0)),
        ],
        out_shape=[
            jax.ShapeDtypeStruct((bsz, t, LANES), BF16),
            jax.ShapeDtypeStruct((bsz, nk, t, tk), BF16),
        ],
        scratch_shapes=[pltpu.VMEM((tq, LANES), jnp.int32), pltpu.VMEM((nk, tq, tk), jnp.int32)],
        compiler_params=_params(),
        name="dsa_index",
    )(iq, misc, misc, misc_meta)


def _masked_flash_update(q, k, v, sel, m, l, acc):
    s = _dot_nt(q, k) * (B_DIM ** -0.5)
    s = jnp.where(sel, s, NEG)
    m_new = jnp.maximum(m, jnp.max(s, axis=-1, keepdims=True))
    alpha = jnp.exp(m - m_new)
    p = jnp.where(sel, jnp.exp(s - m_new), 0.0)
    l = alpha * l + jnp.sum(p, axis=-1, keepdims=True)
    acc = alpha * acc + _dot(p.astype(BF16), v)
    return m_new, l, acc


def _dsa_attn_body(q_ref, k_ref, v_ref, km_ref, vm_ref, mm_ref, mf_ref, o_ref, m_scr, l_scr, acc_scr, *,
                   tq, tk, nk):
    qi = pl.program_id(1)
    kj = pl.program_id(2)
    last = (qi * tq + tq - 1) // tk

    @pl.when(kj == 0)
    def _():
        sel = mm_ref[0].astype(F32) > 0.5
        for h in range(N_HEADS):
            sl = slice(h * LANES, (h + 1) * LANES)
            m0 = jnp.full((tq, 1), NEG, F32)
            l0 = jnp.zeros((tq, 1), F32)
            a0 = jnp.zeros((tq, LANES), F32)
            m, l, acc = _masked_flash_update(q_ref[0, :, sl].astype(BF16), km_ref[:, sl].astype(BF16),
                                             vm_ref[:, sl].astype(BF16), sel, m0, l0, a0)
            m_scr[h] = m
            l_scr[h] = l
            acc_scr[:, sl] = acc

    @pl.when(kj <= last)
    def _():
        sel = mf_ref[0, 0].astype(F32) > 0.5
        for h in range(N_HEADS):
            sl = slice(h * LANES, (h + 1) * LANES)
            m, l, acc = _masked_flash_update(q_ref[0, :, sl].astype(BF16), k_ref[0, :, sl].astype(BF16),
                                             v_ref[0, :, sl].astype(BF16), sel,
                                             m_scr[h], l_scr[h], acc_scr[:, sl])
            m_scr[h] = m
            l_scr[h] = l
            acc_scr[:, sl] = acc

    @pl.when(kj == nk - 1)
    def _():
        for h in range(N_HEADS):
            sl = slice(h * LANES, (h + 1) * LANES)
            o_ref[0, :, sl] = acc_scr[:, sl] * (1.0 / l_scr[h])


def _dsa_attn(bq, bk, bv, bkm, bvm, mask_m, mask_f, *, tq):
    bsz, t, width = bq.shape
    nk, tk = mask_f.shape[1], mask_f.shape[3]
    last = lambda i: (i * tq + tq - 1) // tk
    q_map = lambda b, i, j: (b, i, 0)
    kv_map = lambda b, i, j: (b, jnp.minimum(j, last(i)), 0)
    const = lambda b, i, j: (0, 0)
    return pl.pallas_call(
        functools.partial(_dsa_attn_body, tq=tq, tk=tk, nk=nk),
        grid=(bsz, t // tq, nk),
        in_specs=[
            pl.BlockSpec((1, tq, width), q_map),
            pl.BlockSpec((1, tk, width), kv_map),
            pl.BlockSpec((1, tk, width), kv_map),
            pl.BlockSpec((META_PAD, width), const),
            pl.BlockSpec((META_PAD, width), const),
            pl.BlockSpec((1, tq, LANES), q_map),
            pl.BlockSpec((1, 1, tq, tk), lambda b, i, j: (b, jnp.minimum(j, last(i)), i, 0)),
        ],
        out_specs=pl.BlockSpec((1, tq, width), q_map),
        out_shape=jax.ShapeDtypeStruct(bq.shape, F32),
        scratch_shapes=[pltpu.VMEM((N_HEADS, tq, 1), F32), pltpu.VMEM((N_HEADS, tq, 1), F32),
                        pltpu.VMEM((tq, width), F32)],
        compiler_params=_params(),
        name="dsa_attn_prompt",
    )(bq, bk, bv, bkm, bvm, mask_m, mask_f)


def _dsa_sample_body(iq_ref, wq_ref, kidx_ref, q_ref, kc_ref, vc_ref, kn_ref, vn_ref, o_ref, *,
                     s_new, past, n_top, idx_bits):
    qs = _stack_index_queries(iq_ref[...], IDX_DIM)
    w = wq_ref[...]
    ik_past = kidx_ref[0, N_META:N_META + past, :]
    ik_small = jnp.concatenate(
        [kidx_ref[0, 0:N_META, :], w[:, 0:IDX_DIM], jnp.zeros((META_PAD - N_META - s_new, IDX_DIM), F32)], axis=0)
    lane_s = lax.broadcasted_iota(jnp.int32, (s_new, META_PAD), 1)
    sc_s = _index_scores(qs, w, ik_small, s_new)
    key_s = _sort_key(jnp.where(lane_s < N_META + s_new, sc_s, -jnp.inf))
    key_p = _sort_key(_index_scores(qs, w, ik_past, s_new))
    col_s = jnp.where(lane_s < N_META, lane_s, lane_s + past)
    col_p = N_META + lax.broadcasted_iota(jnp.int32, (s_new, past), 1)

    count_fn = lambda f: _count(f(key_s, col_s)) + _count(f(key_p, col_p))
    thr, cut = _select_threshold(count_fn, s_new, n_top, idx_bits)
    sel_s = _selected(key_s, col_s, thr, cut) > 0.5
    sel_p = _selected(key_p, col_p, thr, cut) > 0.5

    for h in range(N_HEADS):
        sl = slice(h * LANES, (h + 1) * LANES)
        q = q_ref[:, sl].astype(BF16)
        m = jnp.full((s_new, 1), NEG, F32)
        l = jnp.zeros((s_new, 1), F32)
        acc = jnp.zeros((s_new, LANES), F32)
        m, l, acc = _masked_flash_update(q, _small_keys(kc_ref, kn_ref, sl, s_new).astype(BF16),
                                         _small_keys(vc_ref, vn_ref, sl, s_new).astype(BF16), sel_s, m, l, acc)
        m, l, acc = _masked_flash_update(q, kc_ref[0, N_META:N_META + past, sl].astype(BF16),
                                         vc_ref[0, N_META:N_META + past, sl].astype(BF16), sel_p, m, l, acc)
        o_ref[:, sl] = acc * (1.0 / l)


def _dsa_sample(iq, misc, bq, bk, bv, cache_kidx, cache_k, cache_v, *, s_new, n_top):
    bsz, n_cache, width = cache_k.shape
    past = n_cache - N_META
    idx_bits = (n_cache + s_new - 1).bit_length()
    rows = lambda b: (b, 0)
    cache = lambda b: (b, 0, 0)
    return pl.pallas_call(
        functools.partial(_dsa_sample_body, s_new=s_new, past=past, n_top=n_top, idx_bits=idx_bits),
        grid=(bsz,),
        in_specs=[
            pl.BlockSpec((s_new, IQ_WIDTH), rows),
            pl.BlockSpec((s_new, LANES), rows),
            pl.BlockSpec((1, n_cache, IDX_DIM), cache),
            pl.BlockSpec((s_new, width), rows),
            pl.BlockSpec((1, n_cache, width), cache),
            pl.BlockSpec((1, n_cache, width), cache),
            pl.BlockSpec((s_new, width), rows),
            pl.BlockSpec((s_new, width), rows),
        ],
        out_specs=pl.BlockSpec((s_new, width), rows),
        out_shape=jax.ShapeDtypeStruct(bq.shape, F32),
        compiler_params=_params(),
        name="dsa_sample",
    )(iq, misc, cache_kidx, bq, cache_k, cache_v, bk, bv)


def _merge_body(x_ref, oa_ref, ob_ref, g_ref, wa_ref, wb_ref, wo_ref, o_ref):
    ya = _dot(oa_ref[...].astype(BF16), wa_ref[...])
    yb = _dot(ob_ref[...].astype(BF16), wb_ref[...])
    merged = jax.nn.sigmoid(g_ref[:, :D_MODEL]) * ya + jax.nn.sigmoid(g_ref[:, D_MODEL:]) * yb
    o_ref[...] = x_ref[...] + _dot(merged.astype(BF16), wo_ref[...])


def _merge(x, oa, ob, gates, wa, wb, wo, *, tm):
    n, d = x.shape
    row = lambda i: (i, 0)
    const = lambda i: (0, 0)
    return pl.pallas_call(
        _merge_body,
        grid=(n // tm,),
        in_specs=[
            pl.BlockSpec((tm, d), row),
            pl.BlockSpec((tm, d), row),
            pl.BlockSpec((tm, d), row),
            pl.BlockSpec((tm, 2 * d), row),
            pl.BlockSpec(wa.shape, const),
            pl.BlockSpec(wb.shape, const),
            pl.BlockSpec(wo.shape, const),
        ],
        out_specs=pl.BlockSpec((tm, d), row),
        out_shape=jax.ShapeDtypeStruct((n, d), F32),
        compiler_params=_params(),
        name="merge",
    )(x, oa, ob, gates, wa, wb, wo)


def _row_tile(n, pref):
    tm = min(pref, n)
    assert n % tm == 0, (n, tm)
    return tm


def kernel(x_prompt, x_sample, cache_a_k, cache_a_v, cache_b_k, cache_b_v, cache_b_kidx, meta, g_ffn1, w1_gate,
           w1_up, w1_down, g_mix, w_in, lam_q1, lam_k1, lam_q2, lam_k2, a_subln, w_a, w_b, w_o, g_ffn2, w2_gate,
           w2_up, w2_down, g_final):
    bsz, seq, d = x_prompt.shape
    dec_b, dec_s, _ = x_sample.shape
    n_cache = cache_a_k.shape[2]
    past = n_cache - N_META
    assert d == D_MODEL and meta.shape == (N_META, D_MODEL)
    assert cache_a_k.shape[0] == 1, "single-layer step"
    assert past % LANES == 0 and (past % CHUNK) + dec_s <= CHUNK, "all cached and new keys visible to every new query"
    n_top_p = min(TOPK_MAX, seq // 4)
    n_top_s = min(TOPK_MAX, (past + dec_s) // 4)

    cast = lambda w: w[0].astype(BF16)
    w1g, w1u, w1d = cast(w1_gate), cast(w1_up), cast(w1_down)
    w2g, w2u, w2d = cast(w2_gate), cast(w2_up), cast(w2_down)
    wa, wb, wo = cast(w_a), cast(w_b), cast(w_o)
    win = w_in[0]
    c0 = 3 * A_WIDTH
    c1 = c0 + 3 * B_WIDTH
    c2 = c1 + IQ_WIDTH
    c3 = c2 + IDX_DIM + IDX_HEADS
    w_pa = win[:, :c0].astype(BF16)
    w_pb = win[:, c0:c1].astype(BF16)
    w_pig = jnp.concatenate(
        [win[:, c1:c3], jnp.zeros((d, LANES - IDX_DIM - IDX_HEADS), F32), win[:, c3:]], axis=1).astype(BF16)
    g1, gm, g2, gf = g_ffn1[0][None], g_mix[0][None], g_ffn2[0][None], g_final[None]
    lam4 = jnp.stack([lam_q1[0], lam_k1[0], lam_q2[0], lam_k2[0]]).astype(F32)
    gsub = a_subln[0][None]

    def front(x, pos, tm_ffn, tm_proj):
        x1, h = _ffn(x, g1, w1g, w1u, w1d, gm, tm=tm_ffn, emit_h=True)
        return (x1,) + tuple(_project_all(h, pos, w_pa, w_pb, w_pig, tm=tm_proj))

    def back(x1, oa, ob, gates, tm):
        x2 = _merge(x1, oa, ob, gates, wa, wb, wo, tm=tm)
        (y,) = _ffn(x2, g2, w2g, w2u, w2d, gf, tm=tm, emit_h=False)
        return y

    _, _, ak_m, av_m, _, bk_m, bv_m, _, misc_m, _ = front(
        meta, jnp.arange(N_META, dtype=jnp.int32), N_META, N_META)
    pad_meta = lambda a: jnp.pad(a, ((0, META_PAD - N_META), (0, 0)))

    n_p = bsz * seq
    tm_p = _row_tile(n_p, 512)
    pos_p = N_META + jnp.tile(jnp.arange(seq, dtype=jnp.int32), bsz)
    x1_p, aq_p, ak_p, av_p, bq_p, bk_p, bv_p, iq_p, misc_p, gates_p = front(
        x_prompt.reshape(n_p, d), pos_p, tm_p, tm_p)
    r3 = lambda a: a.reshape(bsz, seq, a.shape[-1])
    tq = _row_tile(seq, 512)
    oa_p = _diff_prompt(r3(aq_p), r3(ak_p), r3(av_p), pad_meta(ak_m), pad_meta(av_m), lam4, gsub, tq=tq)
    mask_m, mask_f = _dsa_index(r3(iq_p), r3(misc_p), pad_meta(misc_m), tq=_row_tile(seq, 256), tk=tq,
                                n_top=n_top_p)
    ob_p = _dsa_attn(r3(bq_p), r3(bk_p), r3(bv_p), pad_meta(bk_m), pad_meta(bv_m), mask_m, mask_f, tq=tq)
    y_p = back(x1_p, oa_p.reshape(n_p, d), ob_p.reshape(n_p, d), gates_p, tm_p).reshape(bsz, seq, d)

    n_s = dec_b * dec_s
    pos_s = N_META + past + jnp.tile(jnp.arange(dec_s, dtype=jnp.int32), dec_b)
    x1_s, aq_s, ak_s, av_s, bq_s, bk_s, bv_s, iq_s, misc_s, gates_s = front(
        x_sample.reshape(n_s, d), pos_s, n_s, n_s)
    c3d = lambda c: c[0].reshape(dec_b, n_cache, -1)
    oa_s = _diff_sample(aq_s, ak_s, av_s, c3d(cache_a_k), c3d(cache_a_v), lam4, gsub, s_new=dec_s)
    ob_s = _dsa_sample(iq_s, misc_s, bq_s, bk_s, bv_s, cache_b_kidx[0], c3d(cache_b_k), c3d(cache_b_v),
                       s_new=dec_s, n_top=n_top_s)
    y_s = back(x1_s, oa_s, ob_s, gates_s, n_s).reshape(dec_b, dec_s, d)

    def rows_p(frames, meta_rows, heads):
        w = frames.shape[-1]
        full = jnp.concatenate([jnp.broadcast_to(meta_rows[None], (bsz, N_META, w)), r3(frames)], axis=1)
        return full.reshape((1, bsz, N_META + seq) + heads)

    def rows_s(a, heads):
        return a.reshape((1, dec_b, dec_s) + heads)

    ha, hb, hi = (A_HEADS, 2 * A_DIM), (B_HEADS, B_DIM), (IDX_DIM,)
    return (
        y_p, y_s,
        rows_p(ak_p, ak_m, ha), rows_p(av_p, av_m, ha), rows_p(bk_p, bk_m, hb), rows_p(bv_p, bv_m, hb),
        rows_p(misc_p[:, :IDX_DIM], misc_m[:, :IDX_DIM], hi),
        rows_s(ak_s, ha), rows_s(av_s, ha), rows_s(bk_s, hb), rows_s(bv_s, hb), rows_s(misc_s[:, :IDX_DIM], hi),
    )
```

```python
import functools

import jax
import jax.numpy as jnp
from jax import lax
from jax.experimental import pallas as pl
from jax.experimental.pallas import tpu as pltpu

D_MODEL = 1024
CHUNK = 64
N_META = 16
ROPE_THETA = 10000.0
EPS = 1e-6
A_HEADS = 8
A_DIM = 64
B_HEADS = 8
B_DIM = 128
IDX_HEADS = 8
IDX_DIM = 64
TOPK_MAX = 256
LAM_INIT = 0.2

LANES = 128
N_HEADS = 8
A_WIDTH = A_HEADS * 2 * A_DIM
B_WIDTH = B_HEADS * B_DIM
IQ_WIDTH = IDX_HEADS * IDX_DIM
META_PAD = LANES
SEARCH_ROWS = 64

VMEM_LIMIT = 56 * 1024 * 1024
NEG = -1e30
INT_MIN = -(2 ** 31)
KEY_NEG_INF = -2139095041
LOG2E = 1.4426950408889634

F32 = jnp.float32
BF16 = jnp.bfloat16

_NT = (((1,), (1,)), ((), ()))


def _params():
    return pltpu.CompilerParams(vmem_limit_bytes=VMEM_LIMIT)


def _dot(a, b):
    return jnp.dot(a, b, preferred_element_type=F32)


def _dot_nt(a, b):
    return lax.dot_general(a, b, _NT, preferred_element_type=F32)


def _rms(x, g):
    ms = jnp.mean(x * x, axis=-1, keepdims=True)
    return x * lax.rsqrt(ms + EPS) * g


def _ffn_body(x_ref, g_ref, wg_ref, wu_ref, wd_ref, gn_ref, *rest, n_ff, emit_h):
    if emit_h:
        o_ref, hn_ref, h_scr, acc_scr = rest
    else:
        o_ref, h_scr, acc_scr = rest
    j = pl.program_id(1)

    @pl.when(j == 0)
    def _():
        h_scr[...] = _rms(x_ref[...], g_ref[...]).astype(BF16)
        acc_scr[...] = jnp.zeros_like(acc_scr)

    h = h_scr[...]
    gate = _dot(h, wg_ref[...])
    up = _dot(h, wu_ref[...])
    act = gate * jax.nn.sigmoid(gate) * up
    acc_scr[...] += _dot(act.astype(BF16), wd_ref[...])

    @pl.when(j == n_ff - 1)
    def _():
        y = x_ref[...] + 0.5 * acc_scr[...]
        if emit_h:
            o_ref[...] = y
            hn_ref[...] = _rms(y, gn_ref[...]).astype(BF16)
        else:
            o_ref[...] = _rms(y, gn_ref[...])


def _ffn(x, g, wg, wu, wd, gn, *, tm, emit_h):
    n, d = x.shape
    dff = wg.shape[1]
    tf = 256 if dff % 256 == 0 else LANES
    n_ff = dff // tf
    row = lambda i, j: (i, 0)
    const = lambda i, j: (0, 0)
    out_shape = [jax.ShapeDtypeStruct((n, d), F32)]
    out_specs = [pl.BlockSpec((tm, d), row)]
    if emit_h:
        out_shape.append(jax.ShapeDtypeStruct((n, d), BF16))
        out_specs.append(pl.BlockSpec((tm, d), row))
    return pl.pallas_call(
        functools.partial(_ffn_body, n_ff=n_ff, emit_h=emit_h),
        grid=(n // tm, n_ff),
        in_specs=[
            pl.BlockSpec((tm, d), row),
            pl.BlockSpec((1, d), const),
            pl.BlockSpec((d, tf), lambda i, j: (0, j)),
            pl.BlockSpec((d, tf), lambda i, j: (0, j)),
            pl.BlockSpec((tf, d), lambda i, j: (j, 0)),
            pl.BlockSpec((1, d), const),
        ],
        out_specs=out_specs,
        out_shape=out_shape,
        scratch_shapes=[pltpu.VMEM((tm, d), BF16), pltpu.VMEM((tm, d), F32)],
        compiler_params=_params(),
        name="ffn_h" if emit_h else "ffn_final",
    )(x, g, wg, wu, wd, gn)


def _rope_block(z, tab_ref, kind, t0):
    cos = tab_ref[:, t0 * LANES:(t0 + 1) * LANES]
    s_a = tab_ref[:, (t0 + 1) * LANES:(t0 + 2) * LANES]
    if kind == "d128":
        return z * cos + pltpu.roll(z, 64, 1) * s_a
    s_b = tab_ref[:, (t0 + 2) * LANES:(t0 + 3) * LANES]
    return z * cos + pltpu.roll(z, 96, 1) * s_a + pltpu.roll(z, 32, 1) * s_b


def _proj_body(h_ref, w_ref, tab_ref, *out_refs, segs):
    h = h_ref[...]
    for out_idx, col0, width, rope in segs:
        cw = 256 if width % 256 == 0 else LANES
        for c in range(0, width, cw):
            z = _dot(h, w_ref[:, col0 + c:col0 + c + cw])
            for b in range(cw // LANES):
                zb = z[:, b * LANES:(b + 1) * LANES]
                if rope is not None:
                    zb = _rope_block(zb, tab_ref, rope[0], rope[1])
                out_refs[out_idx][:, c + b * LANES:c + (b + 1) * LANES] = zb


def _proj(h, w, tab, segs, *, tm, name):
    n, d = h.shape
    widths = [s[2] for s in segs]
    row = lambda i: (i, 0)
    return pl.pallas_call(
        functools.partial(_proj_body, segs=tuple(segs)),
        grid=(n // tm,),
        in_specs=[
            pl.BlockSpec((tm, d), row),
            pl.BlockSpec(w.shape, lambda i: (0, 0)),
            pl.BlockSpec((tm, tab.shape[1]), row),
        ],
        out_specs=[pl.BlockSpec((tm, wd), row) for wd in widths],
        out_shape=[jax.ShapeDtypeStruct((n, wd), F32) for wd in widths],
        compiler_params=_params(),
        name=name,
    )(h, w, tab)


def _rope_tables(pos):
    pos = pos.astype(F32)
    inv32 = ROPE_THETA ** (-jnp.arange(32, dtype=F32) / 32)
    inv64 = ROPE_THETA ** (-jnp.arange(64, dtype=F32) / 64)
    a32 = pos[:, None] * inv32[None, :]
    a64 = pos[:, None] * inv64[None, :]
    c32, s32 = jnp.cos(a32), jnp.sin(a32)
    c64, s64 = jnp.cos(a64), jnp.sin(a64)
    z32 = jnp.zeros_like(c32)
    z64 = jnp.zeros_like(c64)
    cat = lambda *xs: jnp.concatenate(xs, axis=1)
    t64 = cat(cat(c32, c32, c32, c32), cat(-s32, z32, -s32, z32), cat(z32, s32, z32, s32))
    t128 = cat(cat(c64, c64), cat(-s64, s64))
    tmisc = cat(cat(c32, c32, jnp.ones_like(c64)), cat(-s32, z32, z64), cat(z32, s32, z64))
    return t64, t128, tmisc


def _project_all(h, pos, w_a, w_b, w_ig, *, tm):
    t64, t128, tmisc = _rope_tables(pos)
    aq, ak, av = _proj(h, w_a, t64,
                       [(0, 0, A_WIDTH, ("d64", 0)), (1, A_WIDTH, A_WIDTH, ("d64", 0)),
                        (2, 2 * A_WIDTH, A_WIDTH, None)], tm=tm, name="proj_a")
    bq, bk, bv = _proj(h, w_b, t128,
                       [(0, 0, B_WIDTH, ("d128", 0)), (1, B_WIDTH, B_WIDTH, ("d128", 0)),
                        (2, 2 * B_WIDTH, B_WIDTH, None)], tm=tm, name="proj_b")
    iq, misc, gates = _proj(h, w_ig, jnp.concatenate([t64, tmisc], axis=1),
                            [(0, 0, IQ_WIDTH, ("d64", 0)), (1, IQ_WIDTH, LANES, ("d64", 3)),
                             (2, IQ_WIDTH + LANES, 2 * D_MODEL, None)], tm=tm, name="proj_ig")
    return aq, ak, av, bq, bk, bv, iq, misc, gates


def _lam(lam_ref):
    lp = lam_ref[...]
    s1 = jnp.sum(lp[0:1] * lp[1:2], axis=-1, keepdims=True)
    s2 = jnp.sum(lp[2:3] * lp[3:4], axis=-1, keepdims=True)
    return jnp.exp(s1) - jnp.exp(s2) + LAM_INIT


def _stack_diff_queries(q):
    lane = lax.broadcasted_iota(jnp.int32, q.shape, 1)
    q = q * (A_DIM ** -0.5 * LOG2E)
    q1 = jnp.where(lane < A_DIM, q, 0.0)
    q2 = jnp.where(lane >= A_DIM, q, 0.0)
    return jnp.concatenate([q1, q2], axis=0).astype(BF16)


def _diff_finish(acc, l, n, lam, g):
    o1 = acc[:n] * (1.0 / l[:n])
    o2 = acc[n:] * (1.0 / l[n:])
    o = o1 - lam * o2
    return _rms(o, g) * (1.0 - LAM_INIT)


def _diff_prompt_body(lam_ref, q_ref, k_ref, v_ref, km_ref, vm_ref, g_ref, o_ref, *, tq, tk):
    qi = pl.program_id(2)
    qs = _stack_diff_queries(q_ref[0])

    def update(state, k, v, visible):
        m, l, acc = state
        s = _dot_nt(qs, k)
        if visible is not None:
            s = jnp.where(visible, s, NEG)
        m_new = jnp.maximum(m, jnp.max(s, axis=-1, keepdims=True))
        alpha = jnp.exp2(m - m_new)
        p = jnp.exp2(s - m_new)
        return m_new, alpha * l + jnp.sum(p, axis=-1, keepdims=True), alpha * acc + _dot(p.astype(BF16), v)

    def load(start, n):
        return k_ref[0, pl.ds(start, n), :].astype(BF16), v_ref[0, pl.ds(start, n), :].astype(BF16)

    col = lax.broadcasted_iota(jnp.int32, (2 * tq, META_PAD), 1)
    state = (jnp.full((2 * tq, 1), NEG, F32), jnp.zeros((2 * tq, 1), F32), jnp.zeros((2 * tq, LANES), F32))
    state = update(state, km_ref[...].astype(BF16), vm_ref[...].astype(BF16), col < N_META)

    n_wide = (qi * tq) // tk
    state = lax.fori_loop(
        0, n_wide, lambda j, st: update(st, *load(pl.multiple_of(j * tk, tk), tk), None), state)
    state = lax.fori_loop(
        n_wide * (tk // tq), qi, lambda j, st: update(st, *load(pl.multiple_of(j * tq, tq), tq), None), state)
    r = lax.broadcasted_iota(jnp.int32, (2 * tq, tq), 0)
    c = lax.broadcasted_iota(jnp.int32, (2 * tq, tq), 1)
    r = jnp.where(r >= tq, r - tq, r)
    m, l, acc = update(state, *load(pl.multiple_of(qi * tq, tq), tq), c // CHUNK <= r // CHUNK)
    o_ref[0] = _diff_finish(acc, l, tq, _lam(lam_ref), g_ref[...])


def _diff_prompt(aq, ak, av, akm, avm, lam4, g, *, tq):
    bsz, t, _ = aq.shape
    tk = min(4 * tq, t)
    head_q = lambda b, h, i: (b, i, h)
    head_all = lambda b, h, i: (b, 0, h)
    head_meta = lambda b, h, i: (0, h)
    const = lambda b, h, i: (0, 0)
    return pl.pallas_call(
        functools.partial(_diff_prompt_body, tq=tq, tk=tk),
        grid=(bsz, N_HEADS, t // tq),
        in_specs=[
            pl.BlockSpec(lam4.shape, const),
            pl.BlockSpec((1, tq, LANES), head_q),
            pl.BlockSpec((1, t, LANES), head_all),
            pl.BlockSpec((1, t, LANES), head_all),
            pl.BlockSpec((META_PAD, LANES), head_meta),
            pl.BlockSpec((META_PAD, LANES), head_meta),
            pl.BlockSpec((1, LANES), const),
        ],
        out_specs=pl.BlockSpec((1, tq, LANES), head_q),
        out_shape=jax.ShapeDtypeStruct(aq.shape, F32),
        compiler_params=_params(),
        name="diff_attn_prompt",
    )(lam4, aq, ak, av, akm, avm, g)


def _cache_head(cache_ref, h, first, n):
    return cache_ref[0, pl.ds(first * N_HEADS + h, n, stride=N_HEADS), :]


def _small_keys(cache_ref, new_ref, h, s_new):
    meta = _cache_head(cache_ref, h, 0, N_META)
    new = new_ref[:, h * LANES:(h + 1) * LANES]
    pad = jnp.zeros((META_PAD - N_META - s_new, LANES), F32)
    return jnp.concatenate([meta, new, pad], axis=0)


def _diff_sample_body(lam_ref, q_ref, kc_ref, vc_ref, kn_ref, vn_ref, g_ref, o_ref, *, s_new, past):
    lam = _lam(lam_ref)
    for h in range(N_HEADS):
        sl = slice(h * LANES, (h + 1) * LANES)
        qs = _stack_diff_queries(q_ref[:, sl])
        kp = _cache_head(kc_ref, h, N_META, past).astype(BF16)
        vp = _cache_head(vc_ref, h, N_META, past).astype(BF16)
        ks = _small_keys(kc_ref, kn_ref, h, s_new).astype(BF16)
        vs = _small_keys(vc_ref, vn_ref, h, s_new).astype(BF16)
        s_p = _dot_nt(qs, kp)
        s_s = _dot_nt(qs, ks)
        col = lax.broadcasted_iota(jnp.int32, s_s.shape, 1)
        s_s = jnp.where(col < N_META + s_new, s_s, NEG)
        m = jnp.maximum(jnp.max(s_p, axis=-1, keepdims=True), jnp.max(s_s, axis=-1, keepdims=True))
        p_p = jnp.exp2(s_p - m)
        p_s = jnp.exp2(s_s - m)
        l = jnp.sum(p_p, axis=-1, keepdims=True) + jnp.sum(p_s, axis=-1, keepdims=True)
        acc = _dot(p_p.astype(BF16), vp) + _dot(p_s.astype(BF16), vs)
        o_ref[:, sl] = _diff_finish(acc, l, s_new, lam, g_ref[...])


def _diff_sample(aq, ak, av, cache_k, cache_v, lam4, g, *, s_new):
    bsz, n_rows, _ = cache_k.shape
    width = aq.shape[1]
    past = n_rows // N_HEADS - N_META
    rows = lambda b: (b, 0)
    cache = lambda b: (b, 0, 0)
    const = lambda b: (0, 0)
    return pl.pallas_call(
        functools.partial(_diff_sample_body, s_new=s_new, past=past),
        grid=(bsz,),
        in_specs=[
            pl.BlockSpec(lam4.shape, const),
            pl.BlockSpec((s_new, width), rows),
            pl.BlockSpec((1, n_rows, LANES), cache),
            pl.BlockSpec((1, n_rows, LANES), cache),
            pl.BlockSpec((s_new, width), rows),
            pl.BlockSpec((s_new, width), rows),
            pl.BlockSpec((1, LANES), const),
        ],
        out_specs=pl.BlockSpec((s_new, width), rows),
        out_shape=jax.ShapeDtypeStruct(aq.shape, F32),
        compiler_params=_params(),
        name="diff_attn_sample",
    )(lam4, aq, cache_k, cache_v, ak, av, g)


def _stack_index_queries(iq, width):
    n = iq.shape[0]
    lane = lax.broadcasted_iota(jnp.int32, (n, LANES), 1)
    parts = []
    for h in range(IDX_HEADS):
        blk = iq[:, (h // 2) * LANES:(h // 2 + 1) * LANES]
        if h % 2:
            blk = pltpu.roll(blk, IDX_DIM, 1)
        blk = jnp.where(lane < IDX_DIM, blk, 0.0) * (IDX_DIM ** -0.5)
        parts.append(blk[:, :width])
    return jnp.concatenate(parts, axis=0).astype(BF16)


def _index_scores(qs, w, keys, n):
    rel = jnp.maximum(_dot_nt(qs, keys.astype(BF16)), 0.0)
    sc = rel[0:n] * w[:, IDX_DIM:IDX_DIM + 1]
    for h in range(1, IDX_HEADS):
        sc = sc + rel[h * n:(h + 1) * n] * w[:, IDX_DIM + h:IDX_DIM + h + 1]
    return sc * (IDX_HEADS ** -0.5)


def _sort_key(x):
    bits = lax.bitcast_convert_type(x, jnp.int32)
    key = bits ^ ((bits >> 31) & 0x7FFFFFFF)
    return jnp.where(key == -1, 0, key)


def _ind(pred):
    return jnp.where(pred, 1.0, 0.0)


def _lane_sum(x):
    acc = x[:, 0:LANES]
    for b in range(1, x.shape[1] // LANES):
        acc = acc + x[:, b * LANES:(b + 1) * LANES]
    return acc


def _select_threshold(count_fn, n_rows, n_top, idx_bits):
    def bit_step(i, thr):
        cand = thr ^ jnp.left_shift(jnp.int32(1), 31 - i)
        cnt = count_fn(lambda key, col, rows: _ind(key >= cand[rows]))
        return jnp.where(cnt >= n_top, cand, thr)

    thr = lax.fori_loop(0, 32, bit_step, jnp.full((n_rows, 1), INT_MIN, jnp.int32))
    need = n_top - count_fn(lambda key, col, rows: _ind(key > thr[rows]))
    excess = count_fn(lambda key, col, rows: _ind(key == thr[rows])) - need

    def tie_search(_):
        def idx_step(i, cut):
            cand = cut | jnp.left_shift(jnp.int32(1), idx_bits - 1 - i)
            cnt = count_fn(lambda key, col, rows: jnp.where(key == thr[rows], _ind(col < cand[rows]), 0.0))
            return jnp.where(cnt < need, cand, cut)
        return lax.fori_loop(0, idx_bits, idx_step, jnp.zeros((n_rows, 1), jnp.int32))

    no_ties = lambda _: jnp.full((n_rows, 1), 2 ** 30, jnp.int32)
    cut = lax.cond(jnp.max(excess) > 0.0, tie_search, no_ties, None)
    return thr, cut


def _select_bias(key, col, thr, cut):
    tie = jnp.where(key == thr, jnp.where(col <= cut, 0.0, NEG), NEG)
    bias = jnp.where(key > thr, 0.0, tie)
    return jnp.where(key > KEY_NEG_INF, bias, NEG)


def _dsa_index_body(iq_ref, wq_ref, ik_ref, ikm_ref, om_ref, of_ref, key_m, key_f, *, tq, tk, n_tiles_all,
                    n_top, idx_bits):
    qi = pl.program_id(1)
    qs = _stack_index_queries(iq_ref[0], LANES)
    w = wq_ref[0]

    sc = _index_scores(qs, w, ikm_ref[...], tq)
    lane = lax.broadcasted_iota(jnp.int32, (tq, LANES), 1)
    key_m[...] = _sort_key(jnp.where(lane < N_META, sc, -jnp.inf))

    row0 = qi * tq
    n_tiles = (row0 + tq - 1) // tk + 1
    rowc = (row0 + lax.broadcasted_iota(jnp.int32, (tq, tk), 0)) // CHUNK
    colt = lax.broadcasted_iota(jnp.int32, (tq, tk), 1)

    def score_tile(j, _):
        start = pl.multiple_of(j * tk, tk)
        sc = _index_scores(qs, w, ik_ref[0, pl.ds(start, tk), :], tq)
        visible = (start + colt) // CHUNK <= rowc
        key_f[j] = _sort_key(jnp.where(visible, sc, -jnp.inf))
        return 0

    lax.fori_loop(0, n_tiles, score_tile, 0)

    lane_c = lax.broadcasted_iota(jnp.int32, (SEARCH_ROWS, LANES), 1)
    col_c = lax.broadcasted_iota(jnp.int32, (SEARCH_ROWS, tk), 1)

    def count_fn(f):
        accs = []
        for r0 in range(0, tq, SEARCH_ROWS):
            rows = slice(r0, r0 + SEARCH_ROWS)
            accs.append(lax.fori_loop(
                0, n_tiles, lambda j, a: a + _lane_sum(f(key_f[j, rows, :], META_PAD + j * tk + col_c, rows)),
                f(key_m[rows, :], lane_c, rows)))
        return jnp.sum(jnp.concatenate(accs, axis=0), axis=-1, keepdims=True)

    thr, cut = _select_threshold(count_fn, tq, n_top, idx_bits)
    om_ref[0] = _select_bias(key_m[...], lane, thr, cut).astype(BF16)

    def write_tile(j, _):
        of_ref[0, j] = _select_bias(key_f[j], META_PAD + j * tk + colt, thr, cut).astype(BF16)
        return 0

    lax.fori_loop(0, n_tiles, write_tile, 0)

    def fill_tile(j, _):
        of_ref[0, j] = jnp.full((tq, tk), NEG, BF16)
        return 0

    lax.fori_loop(n_tiles, n_tiles_all, fill_tile, 0)


def _dsa_index(iq, misc, misc_meta, *, tq, tk, n_top):
    bsz, t, _ = iq.shape
    nk = t // tk
    idx_bits = (META_PAD + t - 1).bit_length()
    return pl.pallas_call(
        functools.partial(_dsa_index_body, tq=tq, tk=tk, n_tiles_all=nk, n_top=n_top, idx_bits=idx_bits),
        grid=(bsz, t // tq),
        in_specs=[
            pl.BlockSpec((1, tq, IQ_WIDTH), lambda b, i: (b, i, 0)),
            pl.BlockSpec((1, tq, LANES), lambda b, i: (b, i, 0)),
            pl.BlockSpec((1, t, LANES), lambda b, i: (b, 0, 0)),
            pl.BlockSpec((META_PAD, LANES), lambda b, i: (0, 0)),
        ],
        out_specs=[
            pl.BlockSpec((1, tq, LANES), lambda b, i: (b, i, 0)),
            pl.BlockSpec((1, nk, tq, tk), lambda b, i: (b, 0, i, 0)),
        ],
        out_shape=[
            jax.ShapeDtypeStruct((bsz, t, LANES), BF16),
            jax.ShapeDtypeStruct((bsz, nk, t, tk), BF16),
        ],
        scratch_shapes=[pltpu.VMEM((tq, LANES), jnp.int32), pltpu.VMEM((nk, tq, tk), jnp.int32)],
        compiler_params=_params(),
        name="dsa_index",
    )(iq, misc, misc, misc_meta)


def _dsa_query(q):
    return (q * (B_DIM ** -0.5 * LOG2E)).astype(BF16)


def _masked_flash_update(q, k, v, bias, state):
    m, l, acc = state
    s = _dot_nt(q, k) + bias
    m_new = jnp.maximum(m, jnp.max(s, axis=-1, keepdims=True))
    alpha = jnp.exp2(m - m_new)
    p = jnp.exp2(s - m_new)
    return m_new, alpha * l + jnp.sum(p, axis=-1, keepdims=True), alpha * acc + _dot(p.astype(BF16), v)


def _flash_init(n):
    return jnp.full((n, 1), NEG, F32), jnp.zeros((n, 1), F32), jnp.zeros((n, LANES), F32)


def _dsa_attn_body(q_ref, k_ref, v_ref, km_ref, vm_ref, bm_ref, bf_ref, o_ref, *, tq, tb, group):
    qi = pl.program_id(2)
    q = _dsa_query(q_ref[0])

    def tiles(j0, n):
        start = pl.multiple_of(j0 * tb, tb)
        k = k_ref[0, pl.ds(start, n * tb), :].astype(BF16)
        v = v_ref[0, pl.ds(start, n * tb), :].astype(BF16)
        bias = jnp.concatenate([bf_ref[0, j0 + u].astype(F32) for u in range(n)], axis=1)
        return k, v, bias

    state = _masked_flash_update(q, km_ref[...].astype(BF16), vm_ref[...].astype(BF16), bm_ref[0].astype(F32),
                                 _flash_init(tq))
    n_tiles = (qi * tq + tq - 1) // tb + 1
    n_wide = n_tiles // group
    state = lax.fori_loop(
        0, n_wide, lambda g, st: _masked_flash_update(q, *tiles(g * group, group), st), state)
    m, l, acc = lax.fori_loop(
        n_wide * group, n_tiles, lambda j, st: _masked_flash_update(q, *tiles(j, 1), st), state)
    o_ref[0] = acc * (1.0 / l)


def _dsa_attn(bq, bk, bv, bkm, bvm, bias_m, bias_f, *, tq):
    bsz, t, _ = bq.shape
    nb, tb = bias_f.shape[1], bias_f.shape[3]
    group = max(1, min(2048, t) // tb)
    head_q = lambda b, h, i: (b, i, h)
    head_all = lambda b, h, i: (b, 0, h)
    head_meta = lambda b, h, i: (0, h)
    return pl.pallas_call(
        functools.partial(_dsa_attn_body, tq=tq, tb=tb, group=group),
        grid=(bsz, N_HEADS, t // tq),
        in_specs=[
            pl.BlockSpec((1, tq, LANES), head_q),
            pl.BlockSpec((1, t, LANES), head_all),
            pl.BlockSpec((1, t, LANES), head_all),
            pl.BlockSpec((META_PAD, LANES), head_meta),
            pl.BlockSpec((META_PAD, LANES), head_meta),
            pl.BlockSpec((1, tq, LANES), lambda b, h, i: (b, i, 0)),
            pl.BlockSpec((1, nb, tq, tb), lambda b, h, i: (b, 0, i, 0)),
        ],
        out_specs=pl.BlockSpec((1, tq, LANES), head_q),
        out_shape=jax.ShapeDtypeStruct(bq.shape, F32),
        compiler_params=_params(),
        name="dsa_attn_prompt",
    )(bq, bk, bv, bkm, bvm, bias_m, bias_f)


def _dsa_sample_body(iq_ref, wq_ref, kidx_ref, q_ref, kc_ref, vc_ref, kn_ref, vn_ref, o_ref, *,
                     s_new, past, n_top, idx_bits):
    qs = _stack_index_queries(iq_ref[...], IDX_DIM)
    w = wq_ref[...]
    ik_past = kidx_ref[0, N_META:N_META + past, :]
    ik_small = jnp.concatenate(
        [kidx_ref[0, 0:N_META, :], w[:, 0:IDX_DIM], jnp.zeros((META_PAD - N_META - s_new, IDX_DIM), F32)], axis=0)
    lane_s = lax.broadcasted_iota(jnp.int32, (s_new, META_PAD), 1)
    sc_s = _index_scores(qs, w, ik_small, s_new)
    key_s = _sort_key(jnp.where(lane_s < N_META + s_new, sc_s, -jnp.inf))
    key_p = _sort_key(_index_scores(qs, w, ik_past, s_new))
    col_s = jnp.where(lane_s < N_META, lane_s, lane_s + past)
    col_p = N_META + lax.broadcasted_iota(jnp.int32, (s_new, past), 1)

    every = slice(None)
    count_fn = lambda f: jnp.sum(
        f(key_s, col_s, every) + _lane_sum(f(key_p, col_p, every)), axis=-1, keepdims=True)
    thr, cut = _select_threshold(count_fn, s_new, n_top, idx_bits)
    bias_s = _select_bias(key_s, col_s, thr, cut)
    bias_p = _select_bias(key_p, col_p, thr, cut)

    for h in range(N_HEADS):
        sl = slice(h * LANES, (h + 1) * LANES)
        q = _dsa_query(q_ref[:, sl])
        state = _masked_flash_update(q, _small_keys(kc_ref, kn_ref, h, s_new).astype(BF16),
                                     _small_keys(vc_ref, vn_ref, h, s_new).astype(BF16), bias_s,
                                     _flash_init(s_new))
        m, l, acc = _masked_flash_update(q, _cache_head(kc_ref, h, N_META, past).astype(BF16),
                                         _cache_head(vc_ref, h, N_META, past).astype(BF16), bias_p, state)
        o_ref[:, sl] = acc * (1.0 / l)


def _dsa_sample(iq, misc, bq, bk, bv, cache_kidx, cache_k, cache_v, *, s_new, n_top):
    bsz, n_rows, _ = cache_k.shape
    width = bq.shape[1]
    n_cache = n_rows // N_HEADS
    past = n_cache - N_META
    idx_bits = (n_cache + s_new - 1).bit_length()
    rows = lambda b: (b, 0)
    cache = lambda b: (b, 0, 0)
    return pl.pallas_call(
        functools.partial(_dsa_sample_body, s_new=s_new, past=past, n_top=n_top, idx_bits=idx_bits),
        grid=(bsz,),
        in_specs=[
            pl.BlockSpec((s_new, IQ_WIDTH), rows),
            pl.BlockSpec((s_new, LANES), rows),
            pl.BlockSpec((1, n_cache, IDX_DIM), cache),
            pl.BlockSpec((s_new, width), rows),
            pl.BlockSpec((1, n_rows, LANES), cache),
            pl.BlockSpec((1, n_rows, LANES), cache),
            pl.BlockSpec((s_new, width), rows),
            pl.BlockSpec((s_new, width), rows),
        ],
        out_specs=pl.BlockSpec((s_new, width), rows),
        out_shape=jax.ShapeDtypeStruct(bq.shape, F32),
        compiler_params=_params(),
        name="dsa_sample",
    )(iq, misc, cache_kidx, bq, cache_k, cache_v, bk, bv)


def _merge_body(x_ref, oa_ref, ob_ref, g_ref, wa_ref, wb_ref, wo_ref, o_ref):
    ya = _dot(oa_ref[...].astype(BF16), wa_ref[...])
    yb = _dot(ob_ref[...].astype(BF16), wb_ref[...])
    merged = jax.nn.sigmoid(g_ref[:, :D_MODEL]) * ya + jax.nn.sigmoid(g_ref[:, D_MODEL:]) * yb
    o_ref[...] = x_ref[...] + _dot(merged.astype(BF16), wo_ref[...])


def _merge(x, oa, ob, gates, wa, wb, wo, *, tm):
    n, d = x.shape
    row = lambda i: (i, 0)
    const = lambda i: (0, 0)
    return pl.pallas_call(
        _merge_body,
        grid=(n // tm,),
        in_specs=[
            pl.BlockSpec((tm, d), row),
            pl.BlockSpec((tm, d), row),
            pl.BlockSpec((tm, d), row),
            pl.BlockSpec((tm, 2 * d), row),
            pl.BlockSpec(wa.shape, const),
            pl.BlockSpec(wb.shape, const),
            pl.BlockSpec(wo.shape, const),
        ],
        out_specs=pl.BlockSpec((tm, d), row),
        out_shape=jax.ShapeDtypeStruct((n, d), F32),
        compiler_params=_params(),
        name="merge",
    )(x, oa, ob, gates, wa, wb, wo)


def _row_tile(n, pref):
    tm = min(pref, n)
    assert n % tm == 0, (n, tm)
    return tm


def kernel(x_prompt, x_sample, cache_a_k, cache_a_v, cache_b_k, cache_b_v, cache_b_kidx, meta, g_ffn1, w1_gate,
           w1_up, w1_down, g_mix, w_in, lam_q1, lam_k1, lam_q2, lam_k2, a_subln, w_a, w_b, w_o, g_ffn2, w2_gate,
           w2_up, w2_down, g_final):
    bsz, seq, d = x_prompt.shape
    dec_b, dec_s, _ = x_sample.shape
    n_cache = cache_a_k.shape[2]
    past = n_cache - N_META
    assert d == D_MODEL and meta.shape == (N_META, D_MODEL)
    assert cache_a_k.shape[0] == 1, "single-layer step"
    assert past % LANES == 0 and (past % CHUNK) + dec_s <= CHUNK, "all cached and new keys visible to every new query"
    n_top_p = min(TOPK_MAX, seq // 4)
    n_top_s = min(TOPK_MAX, (past + dec_s) // 4)

    cast = lambda w: w[0].astype(BF16)
    w1g, w1u, w1d = cast(w1_gate), cast(w1_up), cast(w1_down)
    w2g, w2u, w2d = cast(w2_gate), cast(w2_up), cast(w2_down)
    wa, wb, wo = cast(w_a), cast(w_b), cast(w_o)
    win = w_in[0]
    c0 = 3 * A_WIDTH
    c1 = c0 + 3 * B_WIDTH
    c2 = c1 + IQ_WIDTH
    c3 = c2 + IDX_DIM + IDX_HEADS
    w_pa = win[:, :c0].astype(BF16)
    w_pb = win[:, c0:c1].astype(BF16)
    w_pig = jnp.concatenate(
        [win[:, c1:c3], jnp.zeros((d, LANES - IDX_DIM - IDX_HEADS), F32), win[:, c3:]], axis=1).astype(BF16)
    g1, gm, g2, gf = g_ffn1[0][None], g_mix[0][None], g_ffn2[0][None], g_final[None]
    lam4 = jnp.stack([lam_q1[0], lam_k1[0], lam_q2[0], lam_k2[0]]).astype(F32)
    gsub = a_subln[0][None]

    def front(x, pos, tm_ffn, tm_proj):
        x1, h = _ffn(x, g1, w1g, w1u, w1d, gm, tm=tm_ffn, emit_h=True)
        return (x1,) + tuple(_project_all(h, pos, w_pa, w_pb, w_pig, tm=tm_proj))

    def back(x1, oa, ob, gates, tm):
        x2 = _merge(x1, oa, ob, gates, wa, wb, wo, tm=tm)
        (y,) = _ffn(x2, g2, w2g, w2u, w2d, gf, tm=tm, emit_h=False)
        return y

    _, _, ak_m, av_m, _, bk_m, bv_m, _, misc_m, _ = front(
        meta, jnp.arange(N_META, dtype=jnp.int32), N_META, N_META)
    pad_meta = lambda a: jnp.pad(a, ((0, META_PAD - N_META), (0, 0)))

    n_p = bsz * seq
    tm_p = _row_tile(n_p, 512)
    pos_p = N_META + jnp.tile(jnp.arange(seq, dtype=jnp.int32), bsz)
    x1_p, aq_p, ak_p, av_p, bq_p, bk_p, bv_p, iq_p, misc_p, gates_p = front(
        x_prompt.reshape(n_p, d), pos_p, tm_p, tm_p)
    r3 = lambda a: a.reshape(bsz, seq, a.shape[-1])
    tq = _row_tile(seq, 512)
    oa_p = _diff_prompt(r3(aq_p), r3(ak_p), r3(av_p), pad_meta(ak_m), pad_meta(av_m), lam4, gsub, tq=tq)
    bias_m, bias_f = _dsa_index(r3(iq_p), r3(misc_p), pad_meta(misc_m), tq=_row_tile(seq, 256), tk=tq,
                                n_top=n_top_p)
    ob_p = _dsa_attn(r3(bq_p), r3(bk_p), r3(bv_p), pad_meta(bk_m), pad_meta(bv_m), bias_m, bias_f, tq=tq)
    y_p = back(x1_p, oa_p.reshape(n_p, d), ob_p.reshape(n_p, d), gates_p, tm_p).reshape(bsz, seq, d)

    n_s = dec_b * dec_s
    pos_s = N_META + past + jnp.tile(jnp.arange(dec_s, dtype=jnp.int32), dec_b)
    x1_s, aq_s, ak_s, av_s, bq_s, bk_s, bv_s, iq_s, misc_s, gates_s = front(
        x_sample.reshape(n_s, d), pos_s, n_s, n_s)
    c3d = lambda c: c[0].reshape(dec_b, n_cache * N_HEADS, LANES)
    oa_s = _diff_sample(aq_s, ak_s, av_s, c3d(cache_a_k), c3d(cache_a_v), lam4, gsub, s_new=dec_s)
    ob_s = _dsa_sample(iq_s, misc_s, bq_s, bk_s, bv_s, cache_b_kidx[0], c3d(cache_b_k), c3d(cache_b_v),
                       s_new=dec_s, n_top=n_top_s)
    y_s = back(x1_s, oa_s, ob_s, gates_s, n_s).reshape(dec_b, dec_s, d)

    def rows_p(frames, meta_rows, heads):
        w = frames.shape[-1]
        full = jnp.concatenate([jnp.broadcast_to(meta_rows[None], (bsz, N_META, w)), r3(frames)], axis=1)
        return full.reshape((1, bsz, N_META + seq) + heads)

    def rows_s(a, heads):
        return a.reshape((1, dec_b, dec_s) + heads)

    ha, hb, hi = (A_HEADS, 2 * A_DIM), (B_HEADS, B_DIM), (IDX_DIM,)
    return (
        y_p, y_s,
        rows_p(ak_p, ak_m, ha), rows_p(av_p, av_m, ha), rows_p(bk_p, bk_m, hb), rows_p(bv_p, bv_m, hb),
        rows_p(misc_p[:, :IDX_DIM], misc_m[:, :IDX_DIM], hi),
        rows_s(ak_s, ha), rows_s(av_s, ha), rows_s(bk_s, hb), rows_s(bv_s, hb), rows_s(misc_s[:, :IDX_DIM], hi),
    )
```

```python
import functools

import jax
import jax.numpy as jnp
from jax import lax
from jax.experimental import pallas as pl
from jax.experimental.pallas import tpu as pltpu

D_MODEL = 1024
CHUNK = 64
N_META = 16
ROPE_THETA = 10000.0
EPS = 1e-6
A_HEADS = 8
A_DIM = 64
B_HEADS = 8
B_DIM = 128
IDX_HEADS = 8
IDX_DIM = 64
TOPK_MAX = 256
LAM_INIT = 0.2

LANES = 128
N_HEADS = 8
A_WIDTH = A_HEADS * 2 * A_DIM
B_WIDTH = B_HEADS * B_DIM
IQ_WIDTH = IDX_HEADS * IDX_DIM
META_PAD = LANES
SEARCH_ROWS = 128

VMEM_LIMIT = 56 * 1024 * 1024
NEG = -1e30
INT_MIN = -(2 ** 31)
KEY_NEG_INF = -2139095041
HI_NEG_INF = KEY_NEG_INF >> 16
I16_MIN, I16_MAX = -(2 ** 15), 2 ** 15 - 1
LOG2E = 1.4426950408889634

F32 = jnp.float32
BF16 = jnp.bfloat16
I16 = jnp.int16

_NT = (((1,), (1,)), ((), ()))


def _params():
    return pltpu.CompilerParams(vmem_limit_bytes=VMEM_LIMIT)


def _dot(a, b):
    return jnp.dot(a, b, preferred_element_type=F32)


def _dot_nt(a, b):
    return lax.dot_general(a, b, _NT, preferred_element_type=F32)


def _rms(x, g):
    ms = jnp.mean(x * x, axis=-1, keepdims=True)
    return x * lax.rsqrt(ms + EPS) * g


def _ffn_body(x_ref, g_ref, wg_ref, wu_ref, wd_ref, gn_ref, *rest, n_ff, emit_h):
    if emit_h:
        o_ref, hn_ref, h_scr, acc_scr = rest
    else:
        o_ref, h_scr, acc_scr = rest
    j = pl.program_id(1)

    @pl.when(j == 0)
    def _():
        h_scr[...] = _rms(x_ref[...], g_ref[...]).astype(BF16)
        acc_scr[...] = jnp.zeros_like(acc_scr)

    h = h_scr[...]
    gate = _dot(h, wg_ref[...])
    up = _dot(h, wu_ref[...])
    act = gate * jax.nn.sigmoid(gate) * up
    acc_scr[...] += _dot(act.astype(BF16), wd_ref[...])

    @pl.when(j == n_ff - 1)
    def _():
        y = x_ref[...] + 0.5 * acc_scr[...]
        if emit_h:
            o_ref[...] = y
            hn_ref[...] = _rms(y, gn_ref[...]).astype(BF16)
        else:
            o_ref[...] = _rms(y, gn_ref[...])


def _ffn(x, g, wg, wu, wd, gn, *, tm, emit_h):
    n, d = x.shape
    dff = wg.shape[1]
    tf = 256 if dff % 256 == 0 else LANES
    n_ff = dff // tf
    row = lambda i, j: (i, 0)
    const = lambda i, j: (0, 0)
    out_shape = [jax.ShapeDtypeStruct((n, d), F32)]
    out_specs = [pl.BlockSpec((tm, d), row)]
    if emit_h:
        out_shape.append(jax.ShapeDtypeStruct((n, d), BF16))
        out_specs.append(pl.BlockSpec((tm, d), row))
    return pl.pallas_call(
        functools.partial(_ffn_body, n_ff=n_ff, emit_h=emit_h),
        grid=(n // tm, n_ff),
        in_specs=[
            pl.BlockSpec((tm, d), row),
            pl.BlockSpec((1, d), const),
            pl.BlockSpec((d, tf), lambda i, j: (0, j)),
            pl.BlockSpec((d, tf), lambda i, j: (0, j)),
            pl.BlockSpec((tf, d), lambda i, j: (j, 0)),
            pl.BlockSpec((1, d), const),
        ],
        out_specs=out_specs,
        out_shape=out_shape,
        scratch_shapes=[pltpu.VMEM((tm, d), BF16), pltpu.VMEM((tm, d), F32)],
        compiler_params=_params(),
        name="ffn_h" if emit_h else "ffn_final",
    )(x, g, wg, wu, wd, gn)


def _rope_block(z, tab_ref, kind, t0):
    cos = tab_ref[:, t0 * LANES:(t0 + 1) * LANES]
    s_a = tab_ref[:, (t0 + 1) * LANES:(t0 + 2) * LANES]
    if kind == "d128":
        return z * cos + pltpu.roll(z, 64, 1) * s_a
    s_b = tab_ref[:, (t0 + 2) * LANES:(t0 + 3) * LANES]
    return z * cos + pltpu.roll(z, 96, 1) * s_a + pltpu.roll(z, 32, 1) * s_b


def _proj_body(h_ref, w_ref, tab_ref, *refs, segs, cache_segs, tm, tiles_per_batch, n_steps):
    n_c = len(cache_segs)
    meta_refs = refs[:n_c]
    out_refs = refs[n_c:n_c + len(segs)]
    cache_refs = refs[n_c + len(segs):2 * n_c + len(segs)]
    if n_c:
        stage, meta_stage, sem, meta_sem = refs[2 * n_c + len(segs):]
    i = pl.program_id(0)
    slot = i % 2
    batch = i // tiles_per_batch
    tile = i % tiles_per_batch

    def frames_copy(c, s):
        first = (N_META + tile * tm) * N_HEADS
        return pltpu.make_async_copy(
            stage.at[c, s], cache_refs[c].at[batch, pl.ds(first, tm * N_HEADS), :], sem.at[c, s])

    if n_c:
        @pl.when(i >= 2)
        def _():
            for c in range(n_c):
                frames_copy(c, slot).wait()

    h = h_ref[...]
    for out_idx, col0, width, rope in segs:
        cw = 256 if width % 256 == 0 else LANES
        for c in range(0, width, cw):
            z = _dot(h, w_ref[:, col0 + c:col0 + c + cw])
            for b in range(cw // LANES):
                zb = z[:, b * LANES:(b + 1) * LANES]
                if rope is not None:
                    zb = _rope_block(zb, tab_ref, rope[0], rope[1])
                out_refs[out_idx][:, c + b * LANES:c + (b + 1) * LANES] = zb
                if out_idx in cache_segs:
                    head = c // LANES + b
                    stage[cache_segs.index(out_idx), slot, pl.ds(head, tm, stride=N_HEADS), :] = zb

    for c in range(n_c):
        frames_copy(c, slot).start()

    if n_c:
        @pl.when(tile == 0)
        def _():
            for c in range(n_c):
                for head in range(N_HEADS):
                    meta_stage[c, pl.ds(head, N_META, stride=N_HEADS), :] = (
                        meta_refs[c][:, head * LANES:(head + 1) * LANES])
                copy = pltpu.make_async_copy(
                    meta_stage.at[c], cache_refs[c].at[batch, pl.ds(0, N_META * N_HEADS), :], meta_sem.at[c])
                copy.start()
                copy.wait()

        @pl.when(i == n_steps - 1)
        def _():
            for c in range(n_c):
                frames_copy(c, slot).wait()
                if n_steps > 1:
                    frames_copy(c, 1 - slot).wait()


def _proj(h, w, tab, segs, *, tm, name, cache=None):
    n, d = h.shape
    widths = [s[2] for s in segs]
    row = lambda i: (i, 0)
    const = lambda i: (0, 0)
    n_steps = n // tm
    seq, meta_rows = cache if cache is not None else (n, {})
    cache_segs = tuple(sorted(meta_rows))
    n_c = len(cache_segs)
    assert seq % tm == 0 and all(widths[s] == N_HEADS * LANES for s in cache_segs)
    out_specs = [pl.BlockSpec((tm, wd), row) for wd in widths]
    out_shape = [jax.ShapeDtypeStruct((n, wd), F32) for wd in widths]
    out_specs += [pl.BlockSpec(memory_space=pl.ANY)] * n_c
    out_shape += [jax.ShapeDtypeStruct((n // seq, (N_META + seq) * N_HEADS, LANES), F32)] * n_c
    scratch = []
    if n_c:
        scratch = [pltpu.VMEM((n_c, 2, tm * N_HEADS, LANES), F32), pltpu.VMEM((n_c, N_META * N_HEADS, LANES), F32),
                   pltpu.SemaphoreType.DMA((n_c, 2)), pltpu.SemaphoreType.DMA((n_c,))]
    return pl.pallas_call(
        functools.partial(_proj_body, segs=tuple(segs), cache_segs=cache_segs, tm=tm,
                          tiles_per_batch=seq // tm, n_steps=n_steps),
        grid=(n_steps,),
        in_specs=[
            pl.BlockSpec((tm, d), row),
            pl.BlockSpec(w.shape, const),
            pl.BlockSpec((tm, tab.shape[1]), row),
        ] + [pl.BlockSpec((N_META, widths[s]), const) for s in cache_segs],
        out_specs=out_specs,
        out_shape=out_shape,
        scratch_shapes=scratch,
        compiler_params=_params(),
        name=name,
    )(h, w, tab, *[meta_rows[s] for s in cache_segs])


def _rope_tables(pos):
    pos = pos.astype(F32)
    inv32 = ROPE_THETA ** (-jnp.arange(32, dtype=F32) / 32)
    inv64 = ROPE_THETA ** (-jnp.arange(64, dtype=F32) / 64)
    a32 = pos[:, None] * inv32[None, :]
    a64 = pos[:, None] * inv64[None, :]
    c32, s32 = jnp.cos(a32), jnp.sin(a32)
    c64, s64 = jnp.cos(a64), jnp.sin(a64)
    z32 = jnp.zeros_like(c32)
    z64 = jnp.zeros_like(c64)
    cat = lambda *xs: jnp.concatenate(xs, axis=1)
    t64 = cat(cat(c32, c32, c32, c32), cat(-s32, z32, -s32, z32), cat(z32, s32, z32, s32))
    t128 = cat(cat(c64, c64), cat(-s64, s64))
    tmisc = cat(cat(c32, c32, jnp.ones_like(c64)), cat(-s32, z32, z64), cat(z32, s32, z64))
    return t64, t128, tmisc


def _project_all(h, pos, w_a, w_b, w_ig, *, tm, cache_meta=None):
    t64, t128, tmisc = _rope_tables(pos)
    cache_a = cache_b = None
    if cache_meta is not None:
        seq, ak_m, av_m, bk_m, bv_m = cache_meta
        cache_a, cache_b = (seq, {1: ak_m, 2: av_m}), (seq, {1: bk_m, 2: bv_m})
    aq, ak, av, *rows_a = _proj(h, w_a, t64,
                                [(0, 0, A_WIDTH, ("d64", 0)), (1, A_WIDTH, A_WIDTH, ("d64", 0)),
                                 (2, 2 * A_WIDTH, A_WIDTH, None)], tm=tm, name="proj_a", cache=cache_a)
    bq, bk, bv, *rows_b = _proj(h, w_b, t128,
                                [(0, 0, B_WIDTH, ("d128", 0)), (1, B_WIDTH, B_WIDTH, ("d128", 0)),
                                 (2, 2 * B_WIDTH, B_WIDTH, None)], tm=tm, name="proj_b", cache=cache_b)
    iq, misc, gates = _proj(h, w_ig, jnp.concatenate([t64, tmisc], axis=1),
                            [(0, 0, IQ_WIDTH, ("d64", 0)), (1, IQ_WIDTH, LANES, ("d64", 3)),
                             (2, IQ_WIDTH + LANES, 2 * D_MODEL, None)], tm=tm, name="proj_ig")
    return (aq, ak, av, bq, bk, bv, iq, misc, gates), tuple(rows_a + rows_b)


def _lam(lam_ref):
    lp = lam_ref[...]
    s1 = jnp.sum(lp[0:1] * lp[1:2], axis=-1, keepdims=True)
    s2 = jnp.sum(lp[2:3] * lp[3:4], axis=-1, keepdims=True)
    return jnp.exp(s1) - jnp.exp(s2) + LAM_INIT


def _stack_diff_queries(q):
    lane = lax.broadcasted_iota(jnp.int32, q.shape, 1)
    q = q * (A_DIM ** -0.5 * LOG2E)
    q1 = jnp.where(lane < A_DIM, q, 0.0)
    q2 = jnp.where(lane >= A_DIM, q, 0.0)
    return jnp.concatenate([q1, q2], axis=0).astype(BF16)


def _diff_finish(acc, l, n, lam, g):
    o1 = acc[:n] * (1.0 / l[:n])
    o2 = acc[n:] * (1.0 / l[n:])
    o = o1 - lam * o2
    return _rms(o, g) * (1.0 - LAM_INIT)


def _diff_prompt_body(lam_ref, q_ref, k_ref, v_ref, km_ref, vm_ref, g_ref, o_ref, *, tq, tk):
    qi = pl.program_id(2)
    qs = _stack_diff_queries(q_ref[0])

    def update(state, k, v, visible):
        m, l, acc = state
        s = _dot_nt(qs, k)
        if visible is not None:
            s = jnp.where(visible, s, NEG)
        m_new = jnp.maximum(m, jnp.max(s, axis=-1, keepdims=True))
        alpha = jnp.exp2(m - m_new)
        p = jnp.exp2(s - m_new)
        return m_new, alpha * l + jnp.sum(p, axis=-1, keepdims=True), alpha * acc + _dot(p.astype(BF16), v)

    def load(start, n):
        return k_ref[0, pl.ds(start, n), :].astype(BF16), v_ref[0, pl.ds(start, n), :].astype(BF16)

    col = lax.broadcasted_iota(jnp.int32, (2 * tq, META_PAD), 1)
    state = (jnp.full((2 * tq, 1), NEG, F32), jnp.zeros((2 * tq, 1), F32), jnp.zeros((2 * tq, LANES), F32))
    state = update(state, km_ref[...].astype(BF16), vm_ref[...].astype(BF16), col < N_META)

    n_wide = (qi * tq) // tk
    state = lax.fori_loop(
        0, n_wide, lambda j, st: update(st, *load(pl.multiple_of(j * tk, tk), tk), None), state)
    state = lax.fori_loop(
        n_wide * (tk // tq), qi, lambda j, st: update(st, *load(pl.multiple_of(j * tq, tq), tq), None), state)
    r = lax.broadcasted_iota(jnp.int32, (2 * tq, tq), 0)
    c = lax.broadcasted_iota(jnp.int32, (2 * tq, tq), 1)
    r = jnp.where(r >= tq, r - tq, r)
    m, l, acc = update(state, *load(pl.multiple_of(qi * tq, tq), tq), c // CHUNK <= r // CHUNK)
    o_ref[0] = _diff_finish(acc, l, tq, _lam(lam_ref), g_ref[...])


def _diff_prompt(aq, ak, av, akm, avm, lam4, g, *, tq):
    bsz, t, _ = aq.shape
    tk = min(4 * tq, t)
    head_q = lambda b, h, i: (b, i, h)
    head_all = lambda b, h, i: (b, 0, h)
    head_meta = lambda b, h, i: (0, h)
    const = lambda b, h, i: (0, 0)
    return pl.pallas_call(
        functools.partial(_diff_prompt_body, tq=tq, tk=tk),
        grid=(bsz, N_HEADS, t // tq),
        in_specs=[
            pl.BlockSpec(lam4.shape, const),
            pl.BlockSpec((1, tq, LANES), head_q),
            pl.BlockSpec((1, t, LANES), head_all),
            pl.BlockSpec((1, t, LANES), head_all),
            pl.BlockSpec((META_PAD, LANES), head_meta),
            pl.BlockSpec((META_PAD, LANES), head_meta),
            pl.BlockSpec((1, LANES), const),
        ],
        out_specs=pl.BlockSpec((1, tq, LANES), head_q),
        out_shape=jax.ShapeDtypeStruct(aq.shape, F32),
        compiler_params=_params(),
        name="diff_attn_prompt",
    )(lam4, aq, ak, av, akm, avm, g)


def _cache_head(cache_ref, h, first, n):
    return cache_ref[0, pl.ds(first * N_HEADS + h, n, stride=N_HEADS), :]


def _small_keys(cache_ref, new_ref, h, s_new):
    meta = _cache_head(cache_ref, h, 0, N_META)
    new = new_ref[:, h * LANES:(h + 1) * LANES]
    pad = jnp.zeros((META_PAD - N_META - s_new, LANES), F32)
    return jnp.concatenate([meta, new, pad], axis=0)


def _diff_sample_body(lam_ref, q_ref, kc_ref, vc_ref, kn_ref, vn_ref, g_ref, o_ref, *, s_new, past):
    lam = _lam(lam_ref)
    for h in range(N_HEADS):
        sl = slice(h * LANES, (h + 1) * LANES)
        qs = _stack_diff_queries(q_ref[:, sl])
        kp = _cache_head(kc_ref, h, N_META, past).astype(BF16)
        vp = _cache_head(vc_ref, h, N_META, past).astype(BF16)
        ks = _small_keys(kc_ref, kn_ref, h, s_new).astype(BF16)
        vs = _small_keys(vc_ref, vn_ref, h, s_new).astype(BF16)
        s_p = _dot_nt(qs, kp)
        s_s = _dot_nt(qs, ks)
        col = lax.broadcasted_iota(jnp.int32, s_s.shape, 1)
        s_s = jnp.where(col < N_META + s_new, s_s, NEG)
        m = jnp.maximum(jnp.max(s_p, axis=-1, keepdims=True), jnp.max(s_s, axis=-1, keepdims=True))
        p_p = jnp.exp2(s_p - m)
        p_s = jnp.exp2(s_s - m)
        l = jnp.sum(p_p, axis=-1, keepdims=True) + jnp.sum(p_s, axis=-1, keepdims=True)
        acc = _dot(p_p.astype(BF16), vp) + _dot(p_s.astype(BF16), vs)
        o_ref[:, sl] = _diff_finish(acc, l, s_new, lam, g_ref[...])


def _diff_sample(aq, ak, av, cache_k, cache_v, lam4, g, *, s_new):
    bsz, n_rows, _ = cache_k.shape
    width = aq.shape[1]
    past = n_rows // N_HEADS - N_META
    rows = lambda b: (b, 0)
    cache = lambda b: (b, 0, 0)
    const = lambda b: (0, 0)
    return pl.pallas_call(
        functools.partial(_diff_sample_body, s_new=s_new, past=past),
        grid=(bsz,),
        in_specs=[
            pl.BlockSpec(lam4.shape, const),
            pl.BlockSpec((s_new, width), rows),
            pl.BlockSpec((1, n_rows, LANES), cache),
            pl.BlockSpec((1, n_rows, LANES), cache),
            pl.BlockSpec((s_new, width), rows),
            pl.BlockSpec((s_new, width), rows),
            pl.BlockSpec((1, LANES), const),
        ],
        out_specs=pl.BlockSpec((s_new, width), rows),
        out_shape=jax.ShapeDtypeStruct(aq.shape, F32),
        compiler_params=_params(),
        name="diff_attn_sample",
    )(lam4, aq, cache_k, cache_v, ak, av, g)


def _stack_index_queries(iq, width):
    n = iq.shape[0]
    lane = lax.broadcasted_iota(jnp.int32, (n, LANES), 1)
    parts = []
    for h in range(IDX_HEADS):
        blk = iq[:, (h // 2) * LANES:(h // 2 + 1) * LANES]
        if h % 2:
            blk = pltpu.roll(blk, IDX_DIM, 1)
        blk = jnp.where(lane < IDX_DIM, blk, 0.0) * (IDX_DIM ** -0.5)
        parts.append(blk[:, :width])
    return jnp.concatenate(parts, axis=0).astype(BF16)


def _index_scores(qs, w, keys, n):
    rel = jnp.maximum(_dot_nt(qs, keys.astype(BF16)), 0.0)
    sc = rel[0:n] * w[:, IDX_DIM:IDX_DIM + 1]
    for h in range(1, IDX_HEADS):
        sc = sc + rel[h * n:(h + 1) * n] * w[:, IDX_DIM + h:IDX_DIM + h + 1]
    return sc * (IDX_HEADS ** -0.5)


def _sort_key(x):
    bits = lax.bitcast_convert_type(x, jnp.int32)
    key = bits ^ ((bits >> 31) & 0x7FFFFFFF)
    return jnp.where(key == -1, 0, key)


def _ind(pred):
    return jnp.where(pred, 1.0, 0.0)


def _lane_sum(x):
    acc = x[:, 0:LANES]
    for b in range(1, x.shape[1] // LANES):
        acc = acc + x[:, b * LANES:(b + 1) * LANES]
    return acc


def _select_threshold(count_fn, n_rows, n_top, idx_bits):
    def bit_step(i, thr):
        cand = thr ^ jnp.left_shift(jnp.int32(1), 31 - i)
        cnt = count_fn(lambda key, col, rows: _ind(key >= cand[rows]))
        return jnp.where(cnt >= n_top, cand, thr)

    thr = lax.fori_loop(0, 32, bit_step, jnp.full((n_rows, 1), INT_MIN, jnp.int32))
    need = n_top - count_fn(lambda key, col, rows: _ind(key > thr[rows]))
    excess = count_fn(lambda key, col, rows: _ind(key == thr[rows])) - need

    def tie_search(_):
        def idx_step(i, cut):
            cand = cut | jnp.left_shift(jnp.int32(1), idx_bits - 1 - i)
            cnt = count_fn(lambda key, col, rows: jnp.where(key == thr[rows], _ind(col < cand[rows]), 0.0))
            return jnp.where(cnt < need, cand, cut)
        return lax.fori_loop(0, idx_bits, idx_step, jnp.zeros((n_rows, 1), jnp.int32))

    no_ties = lambda _: jnp.full((n_rows, 1), 2 ** 30, jnp.int32)
    cut = lax.cond(jnp.max(excess) > 0.0, tie_search, no_ties, None)
    return thr, cut


def _select_bias(key, col, thr, cut):
    tie = jnp.where(key == thr, jnp.where(col <= cut, 0.0, NEG), NEG)
    bias = jnp.where(key > thr, 0.0, tie)
    return jnp.where(key > KEY_NEG_INF, bias, NEG)


def _split_key(key):
    hi = (key >> 16).astype(I16)
    lo = ((key & 0xFFFF) - 32768).astype(I16)
    return hi, lo


def _ind16(pred):
    return jnp.where(pred, jnp.ones(pred.shape, I16), jnp.zeros(pred.shape, I16))


def _to_i16(x):
    return jnp.clip(x, I16_MIN, I16_MAX).astype(I16)


def _dsa_index_body(iq_ref, wq_ref, ik_ref, ikm_ref, om_ref, of_ref, hi_m, lo_m, hi_f, lo_f, *, tq, tk,
                    n_tiles_all, n_top, idx_bits):
    qi = pl.program_id(1)
    qs = _stack_index_queries(iq_ref[0], LANES)
    w = wq_ref[0]

    sc = _index_scores(qs, w, ikm_ref[...], tq)
    lane = lax.broadcasted_iota(jnp.int32, (tq, LANES), 1)
    hi_m[...], lo_m[...] = _split_key(_sort_key(jnp.where(lane < N_META, sc, -jnp.inf)))

    row0 = qi * tq
    n_tiles = (row0 + tq - 1) // tk + 1
    rowc = (row0 + lax.broadcasted_iota(jnp.int32, (tq, tk), 0)) // CHUNK
    colt = lax.broadcasted_iota(jnp.int32, (tq, tk), 1)

    def score_tile(j, _):
        start = pl.multiple_of(j * tk, tk)
        sc = _index_scores(qs, w, ik_ref[0, pl.ds(start, tk), :], tq)
        visible = (start + colt) // CHUNK <= rowc
        hi_f[j], lo_f[j] = _split_key(_sort_key(jnp.where(visible, sc, -jnp.inf)))
        return 0

    lax.fori_loop(0, n_tiles, score_tile, 0)

    col_meta = lax.broadcasted_iota(jnp.int32, (SEARCH_ROWS, LANES), 1).astype(I16)
    col_tile = lax.broadcasted_iota(jnp.int32, (SEARCH_ROWS, tk), 1).astype(I16)

    def count(f):
        accs = []
        for r0 in range(0, tq, SEARCH_ROWS):
            rows = slice(r0, r0 + SEARCH_ROWS)
            accs.append(lax.fori_loop(
                0, n_tiles,
                lambda j, a: a + _lane_sum(
                    f(hi_f[j, rows, :], lo_f[j, rows, :], rows, col_tile, META_PAD + j * tk)),
                f(hi_m[rows, :], lo_m[rows, :], rows, col_meta, 0)))
        return jnp.sum(jnp.concatenate(accs, axis=0).astype(F32), axis=-1, keepdims=True)

    def half_search(f_ge, need):
        def step(i, t):
            cand = t + jnp.left_shift(jnp.int32(1), 15 - i)
            c16 = cand.astype(I16)
            return jnp.where(count(f_ge(c16)) >= need, cand, t)
        return lax.fori_loop(0, 16, step, jnp.full((tq, 1), I16_MIN, jnp.int32)).astype(I16)

    th = half_search(lambda c: lambda hi, lo, rows, col, base: _ind16(hi >= c[rows]), n_top)
    need_lo = n_top - count(lambda hi, lo, rows, col, base: _ind16(hi > th[rows]))

    lo_m[...] = jnp.where(hi_m[...] == th, lo_m[...], I16_MIN)

    def tie_class_tile(j, _):
        lo_f[j] = jnp.where(hi_f[j] == th, lo_f[j], I16_MIN)
        return 0

    lax.fori_loop(0, n_tiles, tie_class_tile, 0)
    tl = half_search(lambda c: lambda hi, lo, rows, col, base: _ind16(lo >= c[rows]), need_lo)
    need = need_lo - count(lambda hi, lo, rows, col, base: _ind16(lo > tl[rows]))
    eq = lambda hi, lo, rows: jnp.logical_and(hi == th[rows], lo == tl[rows])
    excess = count(lambda hi, lo, rows, col, base: _ind16(eq(hi, lo, rows))) - need

    def tie_search(_):
        def idx_step(i, cut):
            cand = cut | jnp.left_shift(jnp.int32(1), idx_bits - 1 - i)
            below = lambda hi, lo, rows, col, base: _ind16(
                jnp.logical_and(eq(hi, lo, rows), col < _to_i16(cand - base)[rows]))
            return jnp.where(count(below) < need, cand, cut)
        return lax.fori_loop(0, idx_bits, idx_step, jnp.zeros((tq, 1), jnp.int32))

    no_ties = lambda _: jnp.full((tq, 1), 2 ** 30, jnp.int32)
    cut = lax.cond(jnp.max(excess) > 0.0, tie_search, no_ties, None)

    def bias(hi, lo, col, base):
        zero = jnp.zeros(hi.shape, BF16)
        neg = jnp.full(hi.shape, NEG, BF16)
        tie = jnp.where(col <= _to_i16(cut - base), zero, neg)
        low = jnp.where(lo > tl, zero, jnp.where(lo == tl, tie, neg))
        sel = jnp.where(hi > th, zero, jnp.where(hi == th, low, neg))
        return jnp.where(hi > HI_NEG_INF, sel, neg)

    om_ref[0] = bias(hi_m[...], lo_m[...], lane.astype(I16), 0)
    col_t = colt.astype(I16)

    def write_tile(j, _):
        of_ref[0, j] = bias(hi_f[j], lo_f[j], col_t, META_PAD + j * tk)
        return 0

    lax.fori_loop(0, n_tiles, write_tile, 0)

    def fill_tile(j, _):
        of_ref[0, j] = jnp.full((tq, tk), NEG, BF16)
        return 0

    lax.fori_loop(n_tiles, n_tiles_all, fill_tile, 0)


def _dsa_index(iq, misc, misc_meta, *, tq, tk, n_top):
    bsz, t, _ = iq.shape
    nk = t // tk
    idx_bits = (META_PAD + t - 1).bit_length()
    return pl.pallas_call(
        functools.partial(_dsa_index_body, tq=tq, tk=tk, n_tiles_all=nk, n_top=n_top, idx_bits=idx_bits),
        grid=(bsz, t // tq),
        in_specs=[
            pl.BlockSpec((1, tq, IQ_WIDTH), lambda b, i: (b, i, 0)),
            pl.BlockSpec((1, tq, LANES), lambda b, i: (b, i, 0)),
            pl.BlockSpec((1, t, LANES), lambda b, i: (b, 0, 0)),
            pl.BlockSpec((META_PAD, LANES), lambda b, i: (0, 0)),
        ],
        out_specs=[
            pl.BlockSpec((1, tq, LANES), lambda b, i: (b, i, 0)),
            pl.BlockSpec((1, nk, tq, tk), lambda b, i: (b, 0, i, 0)),
        ],
        out_shape=[
            jax.ShapeDtypeStruct((bsz, t, LANES), BF16),
            jax.ShapeDtypeStruct((bsz, nk, t, tk), BF16),
        ],
        scratch_shapes=[pltpu.VMEM((tq, LANES), I16), pltpu.VMEM((tq, LANES), I16),
                        pltpu.VMEM((nk, tq, tk), I16), pltpu.VMEM((nk, tq, tk), I16)],
        compiler_params=_params(),
        name="dsa_index",
    )(iq, misc, misc, misc_meta)


def _dsa_query(q):
    return (q * (B_DIM ** -0.5 * LOG2E)).astype(BF16)


def _masked_flash_update(q, k, v, bias, state):
    m, l, acc = state
    s = _dot_nt(q, k) + bias
    m_new = jnp.maximum(m, jnp.max(s, axis=-1, keepdims=True))
    alpha = jnp.exp2(m - m_new)
    p = jnp.exp2(s - m_new)
    return m_new, alpha * l + jnp.sum(p, axis=-1, keepdims=True), alpha * acc + _dot(p.astype(BF16), v)


def _flash_init(n):
    return jnp.full((n, 1), NEG, F32), jnp.zeros((n, 1), F32), jnp.zeros((n, LANES), F32)


def _dsa_attn_body(q_ref, k_ref, v_ref, km_ref, vm_ref, bm_ref, bf_ref, o_ref, *, tq, tb, group):
    qi = pl.program_id(2)
    q = _dsa_query(q_ref[0])

    def tiles(j0, n):
        start = pl.multiple_of(j0 * tb, tb)
        k = k_ref[0, pl.ds(start, n * tb), :].astype(BF16)
        v = v_ref[0, pl.ds(start, n * tb), :].astype(BF16)
        bias = jnp.concatenate([bf_ref[0, j0 + u].astype(F32) for u in range(n)], axis=1)
        return k, v, bias

    state = _masked_flash_update(q, km_ref[...].astype(BF16), vm_ref[...].astype(BF16), bm_ref[0].astype(F32),
                                 _flash_init(tq))
    n_tiles = (qi * tq + tq - 1) // tb + 1
    n_wide = n_tiles // group
    state = lax.fori_loop(
        0, n_wide, lambda g, st: _masked_flash_update(q, *tiles(g * group, group), st), state)
    m, l, acc = lax.fori_loop(
        n_wide * group, n_tiles, lambda j, st: _masked_flash_update(q, *tiles(j, 1), st), state)
    o_ref[0] = acc * (1.0 / l)


def _dsa_attn(bq, bk, bv, bkm, bvm, bias_m, bias_f, *, tq):
    bsz, t, _ = bq.shape
    nb, tb = bias_f.shape[1], bias_f.shape[3]
    group = max(1, min(2048, t) // tb)
    head_q = lambda b, h, i: (b, i, h)
    head_all = lambda b, h, i: (b, 0, h)
    head_meta = lambda b, h, i: (0, h)
    return pl.pallas_call(
        functools.partial(_dsa_attn_body, tq=tq, tb=tb, group=group),
        grid=(bsz, N_HEADS, t // tq),
        in_specs=[
            pl.BlockSpec((1, tq, LANES), head_q),
            pl.BlockSpec((1, t, LANES), head_all),
            pl.BlockSpec((1, t, LANES), head_all),
            pl.BlockSpec((META_PAD, LANES), head_meta),
            pl.BlockSpec((META_PAD, LANES), head_meta),
            pl.BlockSpec((1, tq, LANES), lambda b, h, i: (b, i, 0)),
            pl.BlockSpec((1, nb, tq, tb), lambda b, h, i: (b, 0, i, 0)),
        ],
        out_specs=pl.BlockSpec((1, tq, LANES), head_q),
        out_shape=jax.ShapeDtypeStruct(bq.shape, F32),
        compiler_params=_params(),
        name="dsa_attn_prompt",
    )(bq, bk, bv, bkm, bvm, bias_m, bias_f)


def _dsa_sample_body(iq_ref, wq_ref, kidx_ref, q_ref, kc_ref, vc_ref, kn_ref, vn_ref, o_ref, *,
                     s_new, past, n_top, idx_bits):
    qs = _stack_index_queries(iq_ref[...], IDX_DIM)
    w = wq_ref[...]
    ik_past = kidx_ref[0, N_META:N_META + past, :]
    ik_small = jnp.concatenate(
        [kidx_ref[0, 0:N_META, :], w[:, 0:IDX_DIM], jnp.zeros((META_PAD - N_META - s_new, IDX_DIM), F32)], axis=0)
    lane_s = lax.broadcasted_iota(jnp.int32, (s_new, META_PAD), 1)
    sc_s = _index_scores(qs, w, ik_small, s_new)
    key_s = _sort_key(jnp.where(lane_s < N_META + s_new, sc_s, -jnp.inf))
    key_p = _sort_key(_index_scores(qs, w, ik_past, s_new))
    col_s = jnp.where(lane_s < N_META, lane_s, lane_s + past)
    col_p = N_META + lax.broadcasted_iota(jnp.int32, (s_new, past), 1)

    every = slice(None)
    count_fn = lambda f: jnp.sum(
        f(key_s, col_s, every) + _lane_sum(f(key_p, col_p, every)), axis=-1, keepdims=True)
    thr, cut = _select_threshold(count_fn, s_new, n_top, idx_bits)
    bias_s = _select_bias(key_s, col_s, thr, cut)
    bias_p = _select_bias(key_p, col_p, thr, cut)

    for h in range(N_HEADS):
        sl = slice(h * LANES, (h + 1) * LANES)
        q = _dsa_query(q_ref[:, sl])
        state = _masked_flash_update(q, _small_keys(kc_ref, kn_ref, h, s_new).astype(BF16),
                                     _small_keys(vc_ref, vn_ref, h, s_new).astype(BF16), bias_s,
                                     _flash_init(s_new))
        m, l, acc = _masked_flash_update(q, _cache_head(kc_ref, h, N_META, past).astype(BF16),
                                         _cache_head(vc_ref, h, N_META, past).astype(BF16), bias_p, state)
        o_ref[:, sl] = acc * (1.0 / l)


def _dsa_sample(iq, misc, bq, bk, bv, cache_kidx, cache_k, cache_v, *, s_new, n_top):
    bsz, n_rows, _ = cache_k.shape
    width = bq.shape[1]
    n_cache = n_rows // N_HEADS
    past = n_cache - N_META
    idx_bits = (n_cache + s_new - 1).bit_length()
    rows = lambda b: (b, 0)
    cache = lambda b: (b, 0, 0)
    return pl.pallas_call(
        functools.partial(_dsa_sample_body, s_new=s_new, past=past, n_top=n_top, idx_bits=idx_bits),
        grid=(bsz,),
        in_specs=[
            pl.BlockSpec((s_new, IQ_WIDTH), rows),
            pl.BlockSpec((s_new, LANES), rows),
            pl.BlockSpec((1, n_cache, IDX_DIM), cache),
            pl.BlockSpec((s_new, width), rows),
            pl.BlockSpec((1, n_rows, LANES), cache),
            pl.BlockSpec((1, n_rows, LANES), cache),
            pl.BlockSpec((s_new, width), rows),
            pl.BlockSpec((s_new, width), rows),
        ],
        out_specs=pl.BlockSpec((s_new, width), rows),
        out_shape=jax.ShapeDtypeStruct(bq.shape, F32),
        compiler_params=_params(),
        name="dsa_sample",
    )(iq, misc, cache_kidx, bq, cache_k, cache_v, bk, bv)


def _merge_body(x_ref, oa_ref, ob_ref, g_ref, wa_ref, wb_ref, wo_ref, o_ref):
    ya = _dot(oa_ref[...].astype(BF16), wa_ref[...])
    yb = _dot(ob_ref[...].astype(BF16), wb_ref[...])
    merged = jax.nn.sigmoid(g_ref[:, :D_MODEL]) * ya + jax.nn.sigmoid(g_ref[:, D_MODEL:]) * yb
    o_ref[...] = x_ref[...] + _dot(merged.astype(BF16), wo_ref[...])


def _merge(x, oa, ob, gates, wa, wb, wo, *, tm):
    n, d = x.shape
    row = lambda i: (i, 0)
    const = lambda i: (0, 0)
    return pl.pallas_call(
        _merge_body,
        grid=(n // tm,),
        in_specs=[
            pl.BlockSpec((tm, d), row),
            pl.BlockSpec((tm, d), row),
            pl.BlockSpec((tm, d), row),
            pl.BlockSpec((tm, 2 * d), row),
            pl.BlockSpec(wa.shape, const),
            pl.BlockSpec(wb.shape, const),
            pl.BlockSpec(wo.shape, const),
        ],
        out_specs=pl.BlockSpec((tm, d), row),
        out_shape=jax.ShapeDtypeStruct((n, d), F32),
        compiler_params=_params(),
        name="merge",
    )(x, oa, ob, gates, wa, wb, wo)


def _row_tile(n, pref):
    tm = min(pref, n)
    assert n % tm == 0, (n, tm)
    return tm


def kernel(x_prompt, x_sample, cache_a_k, cache_a_v, cache_b_k, cache_b_v, cache_b_kidx, meta, g_ffn1, w1_gate,
           w1_up, w1_down, g_mix, w_in, lam_q1, lam_k1, lam_q2, lam_k2, a_subln, w_a, w_b, w_o, g_ffn2, w2_gate,
           w2_up, w2_down, g_final):
    bsz, seq, d = x_prompt.shape
    dec_b, dec_s, _ = x_sample.shape
    n_cache = cache_a_k.shape[2]
    past = n_cache - N_META
    assert d == D_MODEL and meta.shape == (N_META, D_MODEL)
    assert cache_a_k.shape[0] == 1, "single-layer step"
    assert past % LANES == 0 and (past % CHUNK) + dec_s <= CHUNK, "all cached and new keys visible to every new query"
    n_top_p = min(TOPK_MAX, seq // 4)
    n_top_s = min(TOPK_MAX, (past + dec_s) // 4)

    cast = lambda w: w[0].astype(BF16)
    w1g, w1u, w1d = cast(w1_gate), cast(w1_up), cast(w1_down)
    w2g, w2u, w2d = cast(w2_gate), cast(w2_up), cast(w2_down)
    wa, wb, wo = cast(w_a), cast(w_b), cast(w_o)
    win = w_in[0]
    c0 = 3 * A_WIDTH
    c1 = c0 + 3 * B_WIDTH
    c2 = c1 + IQ_WIDTH
    c3 = c2 + IDX_DIM + IDX_HEADS
    w_pa = win[:, :c0].astype(BF16)
    w_pb = win[:, c0:c1].astype(BF16)
    w_pig = jnp.concatenate(
        [win[:, c1:c3], jnp.zeros((d, LANES - IDX_DIM - IDX_HEADS), F32), win[:, c3:]], axis=1).astype(BF16)
    g1, gm, g2, gf = g_ffn1[0][None], g_mix[0][None], g_ffn2[0][None], g_final[None]
    lam4 = jnp.stack([lam_q1[0], lam_k1[0], lam_q2[0], lam_k2[0]]).astype(F32)
    gsub = a_subln[0][None]

    def front(x, pos, tm_ffn, tm_proj, cache_meta=None):
        x1, h = _ffn(x, g1, w1g, w1u, w1d, gm, tm=tm_ffn, emit_h=True)
        proj, cache_rows = _project_all(h, pos, w_pa, w_pb, w_pig, tm=tm_proj, cache_meta=cache_meta)
        return (x1,) + proj, cache_rows

    def back(x1, oa, ob, gates, tm, tm_ffn):
        x2 = _merge(x1, oa, ob, gates, wa, wb, wo, tm=tm)
        (y,) = _ffn(x2, g2, w2g, w2u, w2d, gf, tm=tm_ffn, emit_h=False)
        return y

    (_, _, ak_m, av_m, _, bk_m, bv_m, _, misc_m, _), _ = front(
        meta, jnp.arange(N_META, dtype=jnp.int32), N_META, N_META)
    pad_meta = lambda a: jnp.pad(a, ((0, META_PAD - N_META), (0, 0)))

    n_p = bsz * seq
    tm_p = _row_tile(seq, 512)
    tm_ffn_p = _row_tile(n_p, 1024)
    pos_p = N_META + jnp.tile(jnp.arange(seq, dtype=jnp.int32), bsz)
    (x1_p, aq_p, ak_p, av_p, bq_p, bk_p, bv_p, iq_p, misc_p, gates_p), cache_rows_p = front(
        x_prompt.reshape(n_p, d), pos_p, tm_ffn_p, tm_p, cache_meta=(seq, ak_m, av_m, bk_m, bv_m))
    r3 = lambda a: a.reshape(bsz, seq, a.shape[-1])
    tq = _row_tile(seq, 512)
    oa_p = _diff_prompt(r3(aq_p), r3(ak_p), r3(av_p), pad_meta(ak_m), pad_meta(av_m), lam4, gsub, tq=tq)
    bias_m, bias_f = _dsa_index(r3(iq_p), r3(misc_p), pad_meta(misc_m), tq=_row_tile(seq, 256), tk=tq,
                                n_top=n_top_p)
    ob_p = _dsa_attn(r3(bq_p), r3(bk_p), r3(bv_p), pad_meta(bk_m), pad_meta(bv_m), bias_m, bias_f, tq=tq)
    y_p = back(x1_p, oa_p.reshape(n_p, d), ob_p.reshape(n_p, d), gates_p, tm_p, tm_ffn_p).reshape(bsz, seq, d)

    n_s = dec_b * dec_s
    pos_s = N_META + past + jnp.tile(jnp.arange(dec_s, dtype=jnp.int32), dec_b)
    (x1_s, aq_s, ak_s, av_s, bq_s, bk_s, bv_s, iq_s, misc_s, gates_s), _ = front(
        x_sample.reshape(n_s, d), pos_s, n_s, n_s)
    c3d = lambda c: c[0].reshape(dec_b, n_cache * N_HEADS, LANES)
    oa_s = _diff_sample(aq_s, ak_s, av_s, c3d(cache_a_k), c3d(cache_a_v), lam4, gsub, s_new=dec_s)
    ob_s = _dsa_sample(iq_s, misc_s, bq_s, bk_s, bv_s, cache_b_kidx[0], c3d(cache_b_k), c3d(cache_b_v),
                       s_new=dec_s, n_top=n_top_s)
    y_s = back(x1_s, oa_s, ob_s, gates_s, n_s, n_s).reshape(dec_b, dec_s, d)

    ha, hb, hi = (A_HEADS, 2 * A_DIM), (B_HEADS, B_DIM), (IDX_DIM,)
    rows_p = lambda a, heads: a.reshape((1, bsz, N_META + seq) + heads)
    rows_s = lambda a, heads: a.reshape((1, dec_b, dec_s) + heads)
    kidx_p = jnp.concatenate(
        [jnp.broadcast_to(misc_m[None, :, :IDX_DIM], (bsz, N_META, IDX_DIM)), r3(misc_p)[:, :, :IDX_DIM]], axis=1)
    ak_r, av_r, bk_r, bv_r = cache_rows_p
    return (
        y_p, y_s,
        rows_p(ak_r, ha), rows_p(av_r, ha), rows_p(bk_r, hb), rows_p(bv_r, hb), kidx_p[None],
        rows_s(ak_s, ha), rows_s(av_s, ha), rows_s(bk_s, hb), rows_s(bv_s, hb), rows_s(misc_s[:, :IDX_DIM], hi),
    )
```

```python
import functools

import jax
import jax.numpy as jnp
from jax import lax
from jax.experimental import pallas as pl
from jax.experimental.pallas import tpu as pltpu

D_MODEL = 1024
CHUNK = 64
N_META = 16
ROPE_THETA = 10000.0
EPS = 1e-6
A_HEADS = 8
A_DIM = 64
B_HEADS = 8
B_DIM = 128
IDX_HEADS = 8
IDX_DIM = 64
TOPK_MAX = 256
LAM_INIT = 0.2

LANES = 128
N_HEADS = 8
A_WIDTH = A_HEADS * 2 * A_DIM
B_WIDTH = B_HEADS * B_DIM
IQ_WIDTH = IDX_HEADS * IDX_DIM
META_PAD = LANES
SEARCH_ROWS = 128

VMEM_LIMIT = 56 * 1024 * 1024
NEG = -1e30
INT_MIN = -(2 ** 31)
KEY_NEG_INF = -2139095041
HI_NEG_INF = KEY_NEG_INF >> 16
I16_MIN = -(2 ** 15)
LOG2E = 1.4426950408889634

F32 = jnp.float32
BF16 = jnp.bfloat16
I16 = jnp.int16

_NT = (((1,), (1,)), ((), ()))


def _params():
    return pltpu.CompilerParams(vmem_limit_bytes=VMEM_LIMIT)


def _dot(a, b):
    return jnp.dot(a, b, preferred_element_type=F32)


def _dot_nt(a, b):
    return lax.dot_general(a, b, _NT, preferred_element_type=F32)


def _rms(x, g):
    ms = jnp.mean(x * x, axis=-1, keepdims=True)
    return x * lax.rsqrt(ms + EPS) * g


def _ffn_body(x_ref, g_ref, wg_ref, wu_ref, wd_ref, gn_ref, *rest, n_ff, emit_h):
    if emit_h:
        o_ref, hn_ref, h_scr, acc_scr = rest
    else:
        o_ref, h_scr, acc_scr = rest
    j = pl.program_id(1)

    @pl.when(j == 0)
    def _():
        h_scr[...] = _rms(x_ref[...], g_ref[...]).astype(BF16)
        acc_scr[...] = jnp.zeros_like(acc_scr)

    h = h_scr[...]
    gate = _dot(h, wg_ref[...])
    up = _dot(h, wu_ref[...])
    act = gate * jax.nn.sigmoid(gate) * up
    acc_scr[...] += _dot(act.astype(BF16), wd_ref[...])

    @pl.when(j == n_ff - 1)
    def _():
        y = x_ref[...] + 0.5 * acc_scr[...]
        if emit_h:
            o_ref[...] = y
            hn_ref[...] = _rms(y, gn_ref[...]).astype(BF16)
        else:
            o_ref[...] = _rms(y, gn_ref[...])


def _ffn(x, g, wg, wu, wd, gn, *, tm, emit_h):
    n, d = x.shape
    dff = wg.shape[1]
    tf = 256 if dff % 256 == 0 else LANES
    n_ff = dff // tf
    row = lambda i, j: (i, 0)
    const = lambda i, j: (0, 0)
    out_shape = [jax.ShapeDtypeStruct((n, d), F32)]
    out_specs = [pl.BlockSpec((tm, d), row)]
    if emit_h:
        out_shape.append(jax.ShapeDtypeStruct((n, d), BF16))
        out_specs.append(pl.BlockSpec((tm, d), row))
    return pl.pallas_call(
        functools.partial(_ffn_body, n_ff=n_ff, emit_h=emit_h),
        grid=(n // tm, n_ff),
        in_specs=[
            pl.BlockSpec((tm, d), row),
            pl.BlockSpec((1, d), const),
            pl.BlockSpec((d, tf), lambda i, j: (0, j)),
            pl.BlockSpec((d, tf), lambda i, j: (0, j)),
            pl.BlockSpec((tf, d), lambda i, j: (j, 0)),
            pl.BlockSpec((1, d), const),
        ],
        out_specs=out_specs,
        out_shape=out_shape,
        scratch_shapes=[pltpu.VMEM((tm, d), BF16), pltpu.VMEM((tm, d), F32)],
        compiler_params=_params(),
        name="ffn_h" if emit_h else "ffn_final",
    )(x, g, wg, wu, wd, gn)


def _rope_block(z, tab_ref, kind, t0):
    cos = tab_ref[:, t0 * LANES:(t0 + 1) * LANES]
    s_a = tab_ref[:, (t0 + 1) * LANES:(t0 + 2) * LANES]
    if kind == "d128":
        return z * cos + pltpu.roll(z, 64, 1) * s_a
    s_b = tab_ref[:, (t0 + 2) * LANES:(t0 + 3) * LANES]
    return z * cos + pltpu.roll(z, 96, 1) * s_a + pltpu.roll(z, 32, 1) * s_b


def _proj_body(h_ref, w_ref, tab_ref, *refs, segs, cache_segs, tm, tiles_per_batch, n_steps):
    n_c = len(cache_segs)
    meta_refs = refs[:n_c]
    out_refs = refs[n_c:n_c + len(segs)]
    cache_refs = refs[n_c + len(segs):2 * n_c + len(segs)]
    if n_c:
        stage, meta_stage, sem, meta_sem = refs[2 * n_c + len(segs):]
    i = pl.program_id(0)
    slot = i % 2
    batch = i // tiles_per_batch
    tile = i % tiles_per_batch

    def frames_copy(c, s):
        first = (N_META + tile * tm) * N_HEADS
        return pltpu.make_async_copy(
            stage.at[c, s], cache_refs[c].at[batch, pl.ds(first, tm * N_HEADS), :], sem.at[c, s])

    if n_c:
        @pl.when(i >= 2)
        def _():
            for c in range(n_c):
                frames_copy(c, slot).wait()

    h = h_ref[...]
    for out_idx, col0, width, rope in segs:
        cw = 256 if width % 256 == 0 else LANES
        for c in range(0, width, cw):
            z = _dot(h, w_ref[:, col0 + c:col0 + c + cw])
            for b in range(cw // LANES):
                zb = z[:, b * LANES:(b + 1) * LANES]
                if rope is not None:
                    zb = _rope_block(zb, tab_ref, rope[0], rope[1])
                out_refs[out_idx][:, c + b * LANES:c + (b + 1) * LANES] = zb
                if out_idx in cache_segs:
                    head = c // LANES + b
                    stage[cache_segs.index(out_idx), slot, pl.ds(head, tm, stride=N_HEADS), :] = zb

    for c in range(n_c):
        frames_copy(c, slot).start()

    if n_c:
        @pl.when(tile == 0)
        def _():
            for c in range(n_c):
                for head in range(N_HEADS):
                    meta_stage[c, pl.ds(head, N_META, stride=N_HEADS), :] = (
                        meta_refs[c][:, head * LANES:(head + 1) * LANES])
                copy = pltpu.make_async_copy(
                    meta_stage.at[c], cache_refs[c].at[batch, pl.ds(0, N_META * N_HEADS), :], meta_sem.at[c])
                copy.start()
                copy.wait()

        @pl.when(i == n_steps - 1)
        def _():
            for c in range(n_c):
                frames_copy(c, slot).wait()
                if n_steps > 1:
                    frames_copy(c, 1 - slot).wait()


def _proj(h, w, tab, segs, *, tm, name, cache=None):
    n, d = h.shape
    widths = [s[2] for s in segs]
    row = lambda i: (i, 0)
    const = lambda i: (0, 0)
    n_steps = n // tm
    seq, meta_rows = cache if cache is not None else (n, {})
    cache_segs = tuple(sorted(meta_rows))
    n_c = len(cache_segs)
    assert seq % tm == 0 and all(widths[s] == N_HEADS * LANES for s in cache_segs)
    assert tab.shape[0] % tm == 0 and n % tab.shape[0] == 0
    tab_tiles = tab.shape[0] // tm
    out_specs = [pl.BlockSpec((tm, wd), row) for wd in widths]
    out_shape = [jax.ShapeDtypeStruct((n, wd), F32) for wd in widths]
    out_specs += [pl.BlockSpec(memory_space=pl.ANY)] * n_c
    out_shape += [jax.ShapeDtypeStruct((n // seq, (N_META + seq) * N_HEADS, LANES), F32)] * n_c
    scratch = []
    if n_c:
        scratch = [pltpu.VMEM((n_c, 2, tm * N_HEADS, LANES), F32), pltpu.VMEM((n_c, N_META * N_HEADS, LANES), F32),
                   pltpu.SemaphoreType.DMA((n_c, 2)), pltpu.SemaphoreType.DMA((n_c,))]
    return pl.pallas_call(
        functools.partial(_proj_body, segs=tuple(segs), cache_segs=cache_segs, tm=tm,
                          tiles_per_batch=seq // tm, n_steps=n_steps),
        grid=(n_steps,),
        in_specs=[
            pl.BlockSpec((tm, d), row),
            pl.BlockSpec(w.shape, const),
            pl.BlockSpec((tm, tab.shape[1]), lambda i: (i % tab_tiles, 0)),
        ] + [pl.BlockSpec((N_META, widths[s]), const) for s in cache_segs],
        out_specs=out_specs,
        out_shape=out_shape,
        scratch_shapes=scratch,
        compiler_params=_params(),
        name=name,
    )(h, w, tab, *[meta_rows[s] for s in cache_segs])


def _rope_tables(pos):
    pos = pos.astype(F32)
    inv32 = ROPE_THETA ** (-jnp.arange(32, dtype=F32) / 32)
    inv64 = ROPE_THETA ** (-jnp.arange(64, dtype=F32) / 64)
    a32 = pos[:, None] * inv32[None, :]
    a64 = pos[:, None] * inv64[None, :]
    c32, s32 = jnp.cos(a32), jnp.sin(a32)
    c64, s64 = jnp.cos(a64), jnp.sin(a64)
    z32 = jnp.zeros_like(c32)
    z64 = jnp.zeros_like(c64)
    return jnp.concatenate(
        [c32, c32, c32, c32, -s32, z32, -s32, z32, z32, s32, z32, s32,
         c64, c64, -s64, s64,
         c32, c32, jnp.ones_like(c64), -s32, z32, z64, z32, s32, z64],
        axis=1)


TAB_D64, TAB_D128, TAB_MISC = 0, 3, 5


def _project_all(h, pos, w_a, w_b, w_ig, *, tm, cache_meta=None):
    tab = _rope_tables(pos)
    cache_a = cache_b = None
    if cache_meta is not None:
        seq, ak_m, av_m, bk_m, bv_m = cache_meta
        cache_a, cache_b = (seq, {1: ak_m, 2: av_m}), (seq, {1: bk_m, 2: bv_m})
    aq, ak, av, *rows_a = _proj(h, w_a, tab,
                                [(0, 0, A_WIDTH, ("d64", TAB_D64)), (1, A_WIDTH, A_WIDTH, ("d64", TAB_D64)),
                                 (2, 2 * A_WIDTH, A_WIDTH, None)], tm=tm, name="proj_a", cache=cache_a)
    bq, bk, bv, *rows_b = _proj(h, w_b, tab,
                                [(0, 0, B_WIDTH, ("d128", TAB_D128)), (1, B_WIDTH, B_WIDTH, ("d128", TAB_D128)),
                                 (2, 2 * B_WIDTH, B_WIDTH, None)], tm=tm, name="proj_b", cache=cache_b)
    iq, misc, gates = _proj(h, w_ig, tab,
                            [(0, 0, IQ_WIDTH, ("d64", TAB_D64)), (1, IQ_WIDTH, LANES, ("d64", TAB_MISC)),
                             (2, IQ_WIDTH + LANES, 2 * D_MODEL, None)], tm=tm, name="proj_ig")
    return (aq, ak, av, bq, bk, bv, iq, misc, gates), tuple(rows_a + rows_b)


def _lam(lam_ref):
    lp = lam_ref[...]
    s1 = jnp.sum(lp[0:1] * lp[1:2], axis=-1, keepdims=True)
    s2 = jnp.sum(lp[2:3] * lp[3:4], axis=-1, keepdims=True)
    return jnp.exp(s1) - jnp.exp(s2) + LAM_INIT


def _stack_diff_queries(q):
    lane = lax.broadcasted_iota(jnp.int32, q.shape, 1)
    q = q * (A_DIM ** -0.5 * LOG2E)
    q1 = jnp.where(lane < A_DIM, q, 0.0)
    q2 = jnp.where(lane >= A_DIM, q, 0.0)
    return jnp.concatenate([q1, q2], axis=0).astype(BF16)


def _diff_finish(acc, l, n, lam, g):
    o1 = acc[:n] * (1.0 / l[:n])
    o2 = acc[n:] * (1.0 / l[n:])
    o = o1 - lam * o2
    return _rms(o, g) * (1.0 - LAM_INIT)


def _diff_prompt_body(lam_ref, q_ref, k_ref, v_ref, km_ref, vm_ref, g_ref, o_ref, *, tq, tk):
    qi = pl.program_id(2)
    qs = _stack_diff_queries(q_ref[0])

    def update(state, k, v, visible):
        m, l, acc = state
        s = _dot_nt(qs, k)
        if visible is not None:
            s = jnp.where(visible, s, NEG)
        m_new = jnp.maximum(m, jnp.max(s, axis=-1, keepdims=True))
        alpha = jnp.exp2(m - m_new)
        p = jnp.exp2(s - m_new)
        return m_new, alpha * l + jnp.sum(p, axis=-1, keepdims=True), alpha * acc + _dot(p.astype(BF16), v)

    def load(start, n):
        return k_ref[0, pl.ds(start, n), :].astype(BF16), v_ref[0, pl.ds(start, n), :].astype(BF16)

    col = lax.broadcasted_iota(jnp.int32, (2 * tq, META_PAD), 1)
    state = (jnp.full((2 * tq, 1), NEG, F32), jnp.zeros((2 * tq, 1), F32), jnp.zeros((2 * tq, LANES), F32))
    state = update(state, km_ref[...].astype(BF16), vm_ref[...].astype(BF16), col < N_META)

    n_wide = (qi * tq) // tk
    state = lax.fori_loop(
        0, n_wide, lambda j, st: update(st, *load(pl.multiple_of(j * tk, tk), tk), None), state)
    state = lax.fori_loop(
        n_wide * (tk // tq), qi, lambda j, st: update(st, *load(pl.multiple_of(j * tq, tq), tq), None), state)
    r = lax.broadcasted_iota(jnp.int32, (2 * tq, tq), 0)
    c = lax.broadcasted_iota(jnp.int32, (2 * tq, tq), 1)
    r = jnp.where(r >= tq, r - tq, r)
    m, l, acc = update(state, *load(pl.multiple_of(qi * tq, tq), tq), c // CHUNK <= r // CHUNK)
    o_ref[0] = _diff_finish(acc, l, tq, _lam(lam_ref), g_ref[...])


def _diff_prompt(aq, ak, av, akm, avm, lam4, g, *, tq):
    bsz, t, _ = aq.shape
    tk = min(4 * tq, t)
    head_q = lambda b, h, i: (b, i, h)
    head_all = lambda b, h, i: (b, 0, h)
    head_meta = lambda b, h, i: (0, h)
    const = lambda b, h, i: (0, 0)
    return pl.pallas_call(
        functools.partial(_diff_prompt_body, tq=tq, tk=tk),
        grid=(bsz, N_HEADS, t // tq),
        in_specs=[
            pl.BlockSpec(lam4.shape, const),
            pl.BlockSpec((1, tq, LANES), head_q),
            pl.BlockSpec((1, t, LANES), head_all),
            pl.BlockSpec((1, t, LANES), head_all),
            pl.BlockSpec((META_PAD, LANES), head_meta),
            pl.BlockSpec((META_PAD, LANES), head_meta),
            pl.BlockSpec((1, LANES), const),
        ],
        out_specs=pl.BlockSpec((1, tq, LANES), head_q),
        out_shape=jax.ShapeDtypeStruct(aq.shape, F32),
        compiler_params=_params(),
        name="diff_attn_prompt",
    )(lam4, aq, ak, av, akm, avm, g)


def _cache_head(cache_ref, h, first, n):
    return cache_ref[0, pl.ds(first * N_HEADS + h, n, stride=N_HEADS), :]


def _small_keys(cache_ref, new_ref, h, s_new):
    meta = _cache_head(cache_ref, h, 0, N_META)
    new = new_ref[:, h * LANES:(h + 1) * LANES]
    pad = jnp.zeros((META_PAD - N_META - s_new, LANES), F32)
    return jnp.concatenate([meta, new, pad], axis=0)


def _diff_sample_body(lam_ref, q_ref, kc_ref, vc_ref, kn_ref, vn_ref, g_ref, o_ref, *, s_new, past):
    lam = _lam(lam_ref)
    for h in range(N_HEADS):
        sl = slice(h * LANES, (h + 1) * LANES)
        qs = _stack_diff_queries(q_ref[:, sl])
        kp = _cache_head(kc_ref, h, N_META, past).astype(BF16)
        vp = _cache_head(vc_ref, h, N_META, past).astype(BF16)
        ks = _small_keys(kc_ref, kn_ref, h, s_new).astype(BF16)
        vs = _small_keys(vc_ref, vn_ref, h, s_new).astype(BF16)
        s_p = _dot_nt(qs, kp)
        s_s = _dot_nt(qs, ks)
        col = lax.broadcasted_iota(jnp.int32, s_s.shape, 1)
        s_s = jnp.where(col < N_META + s_new, s_s, NEG)
        m = jnp.maximum(jnp.max(s_p, axis=-1, keepdims=True), jnp.max(s_s, axis=-1, keepdims=True))
        p_p = jnp.exp2(s_p - m)
        p_s = jnp.exp2(s_s - m)
        l = jnp.sum(p_p, axis=-1, keepdims=True) + jnp.sum(p_s, axis=-1, keepdims=True)
        acc = _dot(p_p.astype(BF16), vp) + _dot(p_s.astype(BF16), vs)
        o_ref[:, sl] = _diff_finish(acc, l, s_new, lam, g_ref[...])


def _diff_sample(aq, ak, av, cache_k, cache_v, lam4, g, *, s_new):
    bsz, n_rows, _ = cache_k.shape
    width = aq.shape[1]
    past = n_rows // N_HEADS - N_META
    rows = lambda b: (b, 0)
    cache = lambda b: (b, 0, 0)
    const = lambda b: (0, 0)
    return pl.pallas_call(
        functools.partial(_diff_sample_body, s_new=s_new, past=past),
        grid=(bsz,),
        in_specs=[
            pl.BlockSpec(lam4.shape, const),
            pl.BlockSpec((s_new, width), rows),
            pl.BlockSpec((1, n_rows, LANES), cache),
            pl.BlockSpec((1, n_rows, LANES), cache),
            pl.BlockSpec((s_new, width), rows),
            pl.BlockSpec((s_new, width), rows),
            pl.BlockSpec((1, LANES), const),
        ],
        out_specs=pl.BlockSpec((s_new, width), rows),
        out_shape=jax.ShapeDtypeStruct(aq.shape, F32),
        compiler_params=_params(),
        name="diff_attn_sample",
    )(lam4, aq, cache_k, cache_v, ak, av, g)


def _stack_index_queries(iq, width):
    n = iq.shape[0]
    lane = lax.broadcasted_iota(jnp.int32, (n, LANES), 1)
    parts = []
    for h in range(IDX_HEADS):
        blk = iq[:, (h // 2) * LANES:(h // 2 + 1) * LANES]
        if h % 2:
            blk = pltpu.roll(blk, IDX_DIM, 1)
        blk = jnp.where(lane < IDX_DIM, blk, 0.0) * (IDX_DIM ** -0.5)
        parts.append(blk[:, :width])
    return jnp.concatenate(parts, axis=0).astype(BF16)


def _index_scores(qs, w, keys, n):
    rel = jnp.maximum(_dot_nt(qs, keys.astype(BF16)), 0.0)
    sc = rel[0:n] * w[:, IDX_DIM:IDX_DIM + 1]
    for h in range(1, IDX_HEADS):
        sc = sc + rel[h * n:(h + 1) * n] * w[:, IDX_DIM + h:IDX_DIM + h + 1]
    return sc * (IDX_HEADS ** -0.5)


def _sort_key(x):
    bits = lax.bitcast_convert_type(x, jnp.int32)
    key = bits ^ ((bits >> 31) & 0x7FFFFFFF)
    return jnp.where(key == -1, 0, key)


def _ind(pred):
    return jnp.where(pred, 1.0, 0.0)


def _lane_sum(x):
    acc = x[:, 0:LANES]
    for b in range(1, x.shape[1] // LANES):
        acc = acc + x[:, b * LANES:(b + 1) * LANES]
    return acc


def _select_threshold(count_fn, n_rows, n_top, idx_bits):
    def bit_step(i, thr):
        cand = thr ^ jnp.left_shift(jnp.int32(1), 31 - i)
        cnt = count_fn(lambda key, col, rows: _ind(key >= cand[rows]))
        return jnp.where(cnt >= n_top, cand, thr)

    thr = lax.fori_loop(0, 32, bit_step, jnp.full((n_rows, 1), INT_MIN, jnp.int32))
    need = n_top - count_fn(lambda key, col, rows: _ind(key > thr[rows]))
    excess = count_fn(lambda key, col, rows: _ind(key == thr[rows])) - need

    def tie_search(_):
        def idx_step(i, cut):
            cand = cut | jnp.left_shift(jnp.int32(1), idx_bits - 1 - i)
            cnt = count_fn(lambda key, col, rows: jnp.where(key == thr[rows], _ind(col < cand[rows]), 0.0))
            return jnp.where(cnt < need, cand, cut)
        return lax.fori_loop(0, idx_bits, idx_step, jnp.zeros((n_rows, 1), jnp.int32))

    no_ties = lambda _: jnp.full((n_rows, 1), 2 ** 30, jnp.int32)
    cut = lax.cond(jnp.max(excess) > 0.0, tie_search, no_ties, None)
    return thr, cut


def _select_bias(key, col, thr, cut):
    tie = jnp.where(key == thr, jnp.where(col <= cut, 0.0, NEG), NEG)
    bias = jnp.where(key > thr, 0.0, tie)
    return jnp.where(key > KEY_NEG_INF, bias, NEG)


def _split_key(key):
    hi = (key >> 16).astype(I16)
    lo = ((key & 0xFFFF) - 32768).astype(I16)
    return hi, lo


def _ind16(pred):
    return jnp.where(pred, jnp.ones(pred.shape, I16), jnp.zeros(pred.shape, I16))


def _dsa_index_body(iq_ref, wq_ref, ik_ref, ikm_ref, om_ref, of_ref, hi_m, lo_m, hi_f, lo_f, *, tq, tk,
                    n_tiles_all, n_top):
    qi = pl.program_id(1)
    qs = _stack_index_queries(iq_ref[0], LANES)
    w = wq_ref[0]

    sc = _index_scores(qs, w, ikm_ref[...], tq)
    lane = lax.broadcasted_iota(jnp.int32, (tq, LANES), 1)
    hi_m[...], lo_m[...] = _split_key(_sort_key(jnp.where(lane < N_META, sc, -jnp.inf)))

    row0 = qi * tq
    n_tiles = (row0 + tq - 1) // tk + 1
    rowc = (row0 + lax.broadcasted_iota(jnp.int32, (tq, tk), 0)) // CHUNK
    colt = lax.broadcasted_iota(jnp.int32, (tq, tk), 1)

    def score_tile(j, _):
        start = pl.multiple_of(j * tk, tk)
        sc = _index_scores(qs, w, ik_ref[0, pl.ds(start, tk), :], tq)
        visible = (start + colt) // CHUNK <= rowc
        hi_f[j], lo_f[j] = _split_key(_sort_key(jnp.where(visible, sc, -jnp.inf)))
        return 0

    lax.fori_loop(0, n_tiles, score_tile, 0)

    def count(f):
        accs = []
        for r0 in range(0, tq, SEARCH_ROWS):
            rows = slice(r0, r0 + SEARCH_ROWS)
            accs.append(lax.fori_loop(
                0, n_tiles, lambda j, a: a + _lane_sum(f(hi_f[j, rows, :], lo_f[j, rows, :], rows)),
                f(hi_m[rows, :], lo_m[rows, :], rows)))
        return jnp.sum(jnp.concatenate(accs, axis=0).astype(F32), axis=-1, keepdims=True)

    def half_search(f_ge, need):
        def step(i, t):
            cand = t + jnp.left_shift(jnp.int32(1), 15 - i)
            c16 = cand.astype(I16)
            return jnp.where(count(f_ge(c16)) >= need, cand, t)
        return lax.fori_loop(0, 16, step, jnp.full((tq, 1), I16_MIN, jnp.int32)).astype(I16)

    th = half_search(lambda c: lambda hi, lo, rows: _ind16(hi >= c[rows]), n_top)
    need_lo = n_top - count(lambda hi, lo, rows: _ind16(hi > th[rows]))

    lo_m[...] = jnp.where(hi_m[...] == th, lo_m[...], I16_MIN)

    def tie_class_tile(j, _):
        lo_f[j] = jnp.where(hi_f[j] == th, lo_f[j], I16_MIN)
        return 0

    lax.fori_loop(0, n_tiles, tie_class_tile, 0)
    tl = half_search(lambda c: lambda hi, lo, rows: _ind16(lo >= c[rows]), need_lo)
    need = need_lo - count(lambda hi, lo, rows: _ind16(lo > tl[rows]))

    def bias(hi, lo, tri, before):
        one = jnp.ones(hi.shape, BF16)
        zero = jnp.zeros(hi.shape, BF16)
        above = jnp.where(hi > th, one, jnp.where(hi == th, jnp.where(lo > tl, one, zero), zero))
        tie = jnp.where(hi == th, jnp.where(lo == tl, one, zero), zero)
        tie = jnp.where(hi > HI_NEG_INF, tie, zero)
        rank = _dot(tie, tri)
        take = jnp.where(rank + before <= need, 1.0, 0.0).astype(BF16)
        chosen = above + tie * take
        return jnp.where(chosen > 0.5, zero, jnp.full(hi.shape, NEG, BF16)), before + rank[:, -1:]

    def upper_ones(n):
        r = lax.broadcasted_iota(jnp.int32, (n, n), 0)
        c = lax.broadcasted_iota(jnp.int32, (n, n), 1)
        return jnp.where(r <= c, 1.0, 0.0).astype(BF16)

    om_ref[0], ties = bias(hi_m[...], lo_m[...], upper_ones(META_PAD), jnp.zeros((tq, 1), F32))
    tri = upper_ones(tk)

    def write_tile(j, before):
        of_ref[0, j], after = bias(hi_f[j], lo_f[j], tri, before)
        return after

    lax.fori_loop(0, n_tiles, write_tile, ties)

    def fill_tile(j, _):
        of_ref[0, j] = jnp.full((tq, tk), NEG, BF16)
        return 0

    lax.fori_loop(n_tiles, n_tiles_all, fill_tile, 0)


def _dsa_index(iq, misc, misc_meta, *, tq, tk, n_top):
    bsz, t, _ = iq.shape
    nk = t // tk
    return pl.pallas_call(
        functools.partial(_dsa_index_body, tq=tq, tk=tk, n_tiles_all=nk, n_top=n_top),
        grid=(bsz, t // tq),
        in_specs=[
            pl.BlockSpec((1, tq, IQ_WIDTH), lambda b, i: (b, i, 0)),
            pl.BlockSpec((1, tq, LANES), lambda b, i: (b, i, 0)),
            pl.BlockSpec((1, t, LANES), lambda b, i: (b, 0, 0)),
            pl.BlockSpec((META_PAD, LANES), lambda b, i: (0, 0)),
        ],
        out_specs=[
            pl.BlockSpec((1, tq, LANES), lambda b, i: (b, i, 0)),
            pl.BlockSpec((1, nk, tq, tk), lambda b, i: (b, 0, i, 0)),
        ],
        out_shape=[
            jax.ShapeDtypeStruct((bsz, t, LANES), BF16),
            jax.ShapeDtypeStruct((bsz, nk, t, tk), BF16),
        ],
        scratch_shapes=[pltpu.VMEM((tq, LANES), I16), pltpu.VMEM((tq, LANES), I16),
                        pltpu.VMEM((nk, tq, tk), I16), pltpu.VMEM((nk, tq, tk), I16)],
        compiler_params=_params(),
        name="dsa_index",
    )(iq, misc, misc, misc_meta)


def _dsa_query(q):
    return (q * (B_DIM ** -0.5 * LOG2E)).astype(BF16)


def _masked_flash_update(q, k, v, bias, state):
    m, l, acc = state
    s = _dot_nt(q, k) + bias
    m_new = jnp.maximum(m, jnp.max(s, axis=-1, keepdims=True))
    alpha = jnp.exp2(m - m_new)
    p = jnp.exp2(s - m_new)
    return m_new, alpha * l + jnp.sum(p, axis=-1, keepdims=True), alpha * acc + _dot(p.astype(BF16), v)


def _flash_init(n):
    return jnp.full((n, 1), NEG, F32), jnp.zeros((n, 1), F32), jnp.zeros((n, LANES), F32)


def _dsa_attn_body(q_ref, k_ref, v_ref, km_ref, vm_ref, bm_ref, bf_ref, o_ref, *, tq, tb, group):
    qi = pl.program_id(2)
    q = _dsa_query(q_ref[0])

    def tiles(j0, n):
        start = pl.multiple_of(j0 * tb, tb)
        k = k_ref[0, pl.ds(start, n * tb), :].astype(BF16)
        v = v_ref[0, pl.ds(start, n * tb), :].astype(BF16)
        bias = jnp.concatenate([bf_ref[0, j0 + u].astype(F32) for u in range(n)], axis=1)
        return k, v, bias

    state = _masked_flash_update(q, km_ref[...].astype(BF16), vm_ref[...].astype(BF16), bm_ref[0].astype(F32),
                                 _flash_init(tq))
    n_tiles = (qi * tq + tq - 1) // tb + 1
    n_wide = n_tiles // group
    state = lax.fori_loop(
        0, n_wide, lambda g, st: _masked_flash_update(q, *tiles(g * group, group), st), state)
    m, l, acc = lax.fori_loop(
        n_wide * group, n_tiles, lambda j, st: _masked_flash_update(q, *tiles(j, 1), st), state)
    o_ref[0] = acc * (1.0 / l)


def _dsa_attn(bq, bk, bv, bkm, bvm, bias_m, bias_f, *, tq):
    bsz, t, _ = bq.shape
    nb, tb = bias_f.shape[1], bias_f.shape[3]
    group = max(1, min(2048, t) // tb)
    head_q = lambda b, h, i: (b, i, h)
    head_all = lambda b, h, i: (b, 0, h)
    head_meta = lambda b, h, i: (0, h)
    return pl.pallas_call(
        functools.partial(_dsa_attn_body, tq=tq, tb=tb, group=group),
        grid=(bsz, N_HEADS, t // tq),
        in_specs=[
            pl.BlockSpec((1, tq, LANES), head_q),
            pl.BlockSpec((1, t, LANES), head_all),
            pl.BlockSpec((1, t, LANES), head_all),
            pl.BlockSpec((META_PAD, LANES), head_meta),
            pl.BlockSpec((META_PAD, LANES), head_meta),
            pl.BlockSpec((1, tq, LANES), lambda b, h, i: (b, i, 0)),
            pl.BlockSpec((1, nb, tq, tb), lambda b, h, i: (b, 0, i, 0)),
        ],
        out_specs=pl.BlockSpec((1, tq, LANES), head_q),
        out_shape=jax.ShapeDtypeStruct(bq.shape, F32),
        compiler_params=_params(),
        name="dsa_attn_prompt",
    )(bq, bk, bv, bkm, bvm, bias_m, bias_f)


def _dsa_sample_body(iq_ref, wq_ref, kidx_ref, q_ref, kc_ref, vc_ref, kn_ref, vn_ref, o_ref, *,
                     s_new, past, n_top, idx_bits):
    qs = _stack_index_queries(iq_ref[...], IDX_DIM)
    w = wq_ref[...]
    ik_past = kidx_ref[0, N_META:N_META + past, :]
    ik_small = jnp.concatenate(
        [kidx_ref[0, 0:N_META, :], w[:, 0:IDX_DIM], jnp.zeros((META_PAD - N_META - s_new, IDX_DIM), F32)], axis=0)
    lane_s = lax.broadcasted_iota(jnp.int32, (s_new, META_PAD), 1)
    sc_s = _index_scores(qs, w, ik_small, s_new)
    key_s = _sort_key(jnp.where(lane_s < N_META + s_new, sc_s, -jnp.inf))
    key_p = _sort_key(_index_scores(qs, w, ik_past, s_new))
    col_s = jnp.where(lane_s < N_META, lane_s, lane_s + past)
    col_p = N_META + lax.broadcasted_iota(jnp.int32, (s_new, past), 1)

    every = slice(None)
    count_fn = lambda f: jnp.sum(
        f(key_s, col_s, every) + _lane_sum(f(key_p, col_p, every)), axis=-1, keepdims=True)
    thr, cut = _select_threshold(count_fn, s_new, n_top, idx_bits)
    bias_s = _select_bias(key_s, col_s, thr, cut)
    bias_p = _select_bias(key_p, col_p, thr, cut)

    for h in range(N_HEADS):
        sl = slice(h * LANES, (h + 1) * LANES)
        q = _dsa_query(q_ref[:, sl])
        state = _masked_flash_update(q, _small_keys(kc_ref, kn_ref, h, s_new).astype(BF16),
                                     _small_keys(vc_ref, vn_ref, h, s_new).astype(BF16), bias_s,
                                     _flash_init(s_new))
        m, l, acc = _masked_flash_update(q, _cache_head(kc_ref, h, N_META, past).astype(BF16),
                                         _cache_head(vc_ref, h, N_META, past).astype(BF16), bias_p, state)
        o_ref[:, sl] = acc * (1.0 / l)


def _dsa_sample(iq, misc, bq, bk, bv, cache_kidx, cache_k, cache_v, *, s_new, n_top):
    bsz, n_rows, _ = cache_k.shape
    width = bq.shape[1]
    n_cache = n_rows // N_HEADS
    past = n_cache - N_META
    idx_bits = (n_cache + s_new - 1).bit_length()
    rows = lambda b: (b, 0)
    cache = lambda b: (b, 0, 0)
    return pl.pallas_call(
        functools.partial(_dsa_sample_body, s_new=s_new, past=past, n_top=n_top, idx_bits=idx_bits),
        grid=(bsz,),
        in_specs=[
            pl.BlockSpec((s_new, IQ_WIDTH), rows),
            pl.BlockSpec((s_new, LANES), rows),
            pl.BlockSpec((1, n_cache, IDX_DIM), cache),
            pl.BlockSpec((s_new, width), rows),
            pl.BlockSpec((1, n_rows, LANES), cache),
            pl.BlockSpec((1, n_rows, LANES), cache),
            pl.BlockSpec((s_new, width), rows),
            pl.BlockSpec((s_new, width), rows),
        ],
        out_specs=pl.BlockSpec((s_new, width), rows),
        out_shape=jax.ShapeDtypeStruct(bq.shape, F32),
        compiler_params=_params(),
        name="dsa_sample",
    )(iq, misc, cache_kidx, bq, cache_k, cache_v, bk, bv)


def _merge_body(x_ref, oa_ref, ob_ref, g_ref, wa_ref, wb_ref, wo_ref, o_ref):
    ya = _dot(oa_ref[...].astype(BF16), wa_ref[...])
    yb = _dot(ob_ref[...].astype(BF16), wb_ref[...])
    merged = jax.nn.sigmoid(g_ref[:, :D_MODEL]) * ya + jax.nn.sigmoid(g_ref[:, D_MODEL:]) * yb
    o_ref[...] = x_ref[...] + _dot(merged.astype(BF16), wo_ref[...])


def _merge(x, oa, ob, gates, wa, wb, wo, *, tm):
    n, d = x.shape
    row = lambda i: (i, 0)
    const = lambda i: (0, 0)
    return pl.pallas_call(
        _merge_body,
        grid=(n // tm,),
        in_specs=[
            pl.BlockSpec((tm, d), row),
            pl.BlockSpec((tm, d), row),
            pl.BlockSpec((tm, d), row),
            pl.BlockSpec((tm, 2 * d), row),
            pl.BlockSpec(wa.shape, const),
            pl.BlockSpec(wb.shape, const),
            pl.BlockSpec(wo.shape, const),
        ],
        out_specs=pl.BlockSpec((tm, d), row),
        out_shape=jax.ShapeDtypeStruct((n, d), F32),
        compiler_params=_params(),
        name="merge",
    )(x, oa, ob, gates, wa, wb, wo)


def _row_tile(n, pref):
    tm = min(pref, n)
    assert n % tm == 0, (n, tm)
    return tm


def kernel(x_prompt, x_sample, cache_a_k, cache_a_v, cache_b_k, cache_b_v, cache_b_kidx, meta, g_ffn1, w1_gate,
           w1_up, w1_down, g_mix, w_in, lam_q1, lam_k1, lam_q2, lam_k2, a_subln, w_a, w_b, w_o, g_ffn2, w2_gate,
           w2_up, w2_down, g_final):
    bsz, seq, d = x_prompt.shape
    dec_b, dec_s, _ = x_sample.shape
    n_cache = cache_a_k.shape[2]
    past = n_cache - N_META
    assert d == D_MODEL and meta.shape == (N_META, D_MODEL)
    assert cache_a_k.shape[0] == 1, "single-layer step"
    assert past % LANES == 0 and (past % CHUNK) + dec_s <= CHUNK, "all cached and new keys visible to every new query"
    n_top_p = min(TOPK_MAX, seq // 4)
    n_top_s = min(TOPK_MAX, (past + dec_s) // 4)

    cast = lambda w: w[0].astype(BF16)
    w1g, w1u, w1d = cast(w1_gate), cast(w1_up), cast(w1_down)
    w2g, w2u, w2d = cast(w2_gate), cast(w2_up), cast(w2_down)
    wa, wb, wo = cast(w_a), cast(w_b), cast(w_o)
    win = w_in[0]
    c0 = 3 * A_WIDTH
    c1 = c0 + 3 * B_WIDTH
    c2 = c1 + IQ_WIDTH
    c3 = c2 + IDX_DIM + IDX_HEADS
    w_pa = win[:, :c0].astype(BF16)
    w_pb = win[:, c0:c1].astype(BF16)
    w_pig = jnp.concatenate(
        [win[:, c1:c3], jnp.zeros((d, LANES - IDX_DIM - IDX_HEADS), F32), win[:, c3:]], axis=1).astype(BF16)
    g1, gm, g2, gf = g_ffn1[0][None], g_mix[0][None], g_ffn2[0][None], g_final[None]
    lam4 = jnp.stack([lam_q1[0], lam_k1[0], lam_q2[0], lam_k2[0]]).astype(F32)
    gsub = a_subln[0][None]

    def front(x, pos, tm_ffn, tm_proj, cache_meta=None):
        x1, h = _ffn(x, g1, w1g, w1u, w1d, gm, tm=tm_ffn, emit_h=True)
        proj, cache_rows = _project_all(h, pos, w_pa, w_pb, w_pig, tm=tm_proj, cache_meta=cache_meta)
        return (x1,) + proj, cache_rows

    def back(x1, oa, ob, gates, tm, tm_ffn):
        x2 = _merge(x1, oa, ob, gates, wa, wb, wo, tm=tm)
        (y,) = _ffn(x2, g2, w2g, w2u, w2d, gf, tm=tm_ffn, emit_h=False)
        return y

    (_, _, ak_m, av_m, _, bk_m, bv_m, _, misc_m, _), _ = front(
        meta, jnp.arange(N_META, dtype=jnp.int32), N_META, N_META)
    pad_meta = lambda a: jnp.pad(a, ((0, META_PAD - N_META), (0, 0)))

    n_p = bsz * seq
    tm_p = _row_tile(seq, 512)
    tm_ffn_p = _row_tile(n_p, 1024)
    pos_p = N_META + jnp.arange(seq, dtype=jnp.int32)
    (x1_p, aq_p, ak_p, av_p, bq_p, bk_p, bv_p, iq_p, misc_p, gates_p), cache_rows_p = front(
        x_prompt.reshape(n_p, d), pos_p, tm_ffn_p, tm_p, cache_meta=(seq, ak_m, av_m, bk_m, bv_m))
    r3 = lambda a: a.reshape(bsz, seq, a.shape[-1])
    tq = _row_tile(seq, 512)
    oa_p = _diff_prompt(r3(aq_p), r3(ak_p), r3(av_p), pad_meta(ak_m), pad_meta(av_m), lam4, gsub, tq=tq)
    bias_m, bias_f = _dsa_index(r3(iq_p), r3(misc_p), pad_meta(misc_m), tq=tq, tk=tq,
                                n_top=n_top_p)
    ob_p = _dsa_attn(r3(bq_p), r3(bk_p), r3(bv_p), pad_meta(bk_m), pad_meta(bv_m), bias_m, bias_f, tq=tq)
    y_p = back(x1_p, oa_p.reshape(n_p, d), ob_p.reshape(n_p, d), gates_p, tm_p, tm_ffn_p).reshape(bsz, seq, d)

    n_s = dec_b * dec_s
    pos_s = N_META + past + jnp.tile(jnp.arange(dec_s, dtype=jnp.int32), dec_b)
    (x1_s, aq_s, ak_s, av_s, bq_s, bk_s, bv_s, iq_s, misc_s, gates_s), _ = front(
        x_sample.reshape(n_s, d), pos_s, n_s, n_s)
    c3d = lambda c: c[0].reshape(dec_b, n_cache * N_HEADS, LANES)
    oa_s = _diff_sample(aq_s, ak_s, av_s, c3d(cache_a_k), c3d(cache_a_v), lam4, gsub, s_new=dec_s)
    ob_s = _dsa_sample(iq_s, misc_s, bq_s, bk_s, bv_s, cache_b_kidx[0], c3d(cache_b_k), c3d(cache_b_v),
                       s_new=dec_s, n_top=n_top_s)
    y_s = back(x1_s, oa_s, ob_s, gates_s, n_s, n_s).reshape(dec_b, dec_s, d)

    ha, hb, hi = (A_HEADS, 2 * A_DIM), (B_HEADS, B_DIM), (IDX_DIM,)
    rows_p = lambda a, heads: a.reshape((1, bsz, N_META + seq) + heads)
    rows_s = lambda a, heads: a.reshape((1, dec_b, dec_s) + heads)
    kidx_p = jnp.concatenate(
        [jnp.broadcast_to(misc_m[None, :, :IDX_DIM], (bsz, N_META, IDX_DIM)), r3(misc_p)[:, :, :IDX_DIM]], axis=1)
    ak_r, av_r, bk_r, bv_r = cache_rows_p
    return (
        y_p, y_s,
        rows_p(ak_r, ha), rows_p(av_r, ha), rows_p(bk_r, hb), rows_p(bv_r, hb), kidx_p[None],
        rows_s(ak_s, ha), rows_s(av_s, ha), rows_s(bk_s, hb), rows_s(bv_s, hb), rows_s(misc_s[:, :IDX_DIM], hi),
    )
```

```python
import functools

import jax
import jax.numpy as jnp
from jax import lax
from jax.experimental import pallas as pl
from jax.experimental.pallas import tpu as pltpu

D_MODEL = 1024
CHUNK = 64
N_META = 16
ROPE_THETA = 10000.0
EPS = 1e-6
A_HEADS = 8
A_DIM = 64
B_HEADS = 8
B_DIM = 128
IDX_HEADS = 8
IDX_DIM = 64
TOPK_MAX = 256
LAM_INIT = 0.2

LANES = 128
N_HEADS = 8
A_WIDTH = A_HEADS * 2 * A_DIM
B_WIDTH = B_HEADS * B_DIM
IQ_WIDTH = IDX_HEADS * IDX_DIM
META_PAD = LANES
SEARCH_ROWS = 128

VMEM_LIMIT = 56 * 1024 * 1024
NEG = -1e30
INT_MIN = -(2 ** 31)
KEY_NEG_INF = -2139095041
HI_NEG_INF = KEY_NEG_INF >> 16
I16_MIN = -(2 ** 15)
LOG2E = 1.4426950408889634

F32 = jnp.float32
BF16 = jnp.bfloat16
I16 = jnp.int16

_NT = (((1,), (1,)), ((), ()))


def _params():
    return pltpu.CompilerParams(vmem_limit_bytes=VMEM_LIMIT)


def _dot(a, b):
    return jnp.dot(a, b, preferred_element_type=F32)


def _dot_nt(a, b):
    return lax.dot_general(a, b, _NT, preferred_element_type=F32)


def _rms(x, g):
    ms = jnp.mean(x * x, axis=-1, keepdims=True)
    return x * lax.rsqrt(ms + EPS) * g


def _ffn_body(x_ref, g_ref, wg_ref, wu_ref, wd_ref, gn_ref, *rest, n_ff, emit_h):
    if emit_h:
        o_ref, hn_ref, h_scr, acc_scr = rest
    else:
        o_ref, h_scr, acc_scr = rest
    j = pl.program_id(1)

    @pl.when(j == 0)
    def _():
        h_scr[...] = _rms(x_ref[...], g_ref[...]).astype(BF16)
        acc_scr[...] = jnp.zeros_like(acc_scr)

    h = h_scr[...]
    gate = _dot(h, wg_ref[...])
    up = _dot(h, wu_ref[...])
    act = gate * jax.nn.sigmoid(gate) * up
    acc_scr[...] += _dot(act.astype(BF16), wd_ref[...])

    @pl.when(j == n_ff - 1)
    def _():
        y = x_ref[...] + 0.5 * acc_scr[...]
        if emit_h:
            o_ref[...] = y
            hn_ref[...] = _rms(y, gn_ref[...]).astype(BF16)
        else:
            o_ref[...] = _rms(y, gn_ref[...])


def _ffn(x, g, wg, wu, wd, gn, *, tm, emit_h):
    n, d = x.shape
    dff = wg.shape[1]
    tf = 256 if dff % 256 == 0 else LANES
    n_ff = dff // tf
    row = lambda i, j: (i, 0)
    const = lambda i, j: (0, 0)
    out_shape = [jax.ShapeDtypeStruct((n, d), F32)]
    out_specs = [pl.BlockSpec((tm, d), row)]
    if emit_h:
        out_shape.append(jax.ShapeDtypeStruct((n, d), BF16))
        out_specs.append(pl.BlockSpec((tm, d), row))
    return pl.pallas_call(
        functools.partial(_ffn_body, n_ff=n_ff, emit_h=emit_h),
        grid=(n // tm, n_ff),
        in_specs=[
            pl.BlockSpec((tm, d), row),
            pl.BlockSpec((1, d), const),
            pl.BlockSpec((d, tf), lambda i, j: (0, j)),
            pl.BlockSpec((d, tf), lambda i, j: (0, j)),
            pl.BlockSpec((tf, d), lambda i, j: (j, 0)),
            pl.BlockSpec((1, d), const),
        ],
        out_specs=out_specs,
        out_shape=out_shape,
        scratch_shapes=[pltpu.VMEM((tm, d), BF16), pltpu.VMEM((tm, d), F32)],
        compiler_params=_params(),
        name="ffn_h" if emit_h else "ffn_final",
    )(x, g, wg, wu, wd, gn)


def _rope_block(z, tab_ref, kind, t0):
    cos = tab_ref[:, t0 * LANES:(t0 + 1) * LANES]
    s_a = tab_ref[:, (t0 + 1) * LANES:(t0 + 2) * LANES]
    if kind == "d128":
        return z * cos + pltpu.roll(z, 64, 1) * s_a
    s_b = tab_ref[:, (t0 + 2) * LANES:(t0 + 3) * LANES]
    return z * cos + pltpu.roll(z, 96, 1) * s_a + pltpu.roll(z, 32, 1) * s_b


def _proj_body(h_ref, w_ref, tab_ref, *refs, segs, cache_segs, tm, tiles_per_batch, n_steps):
    n_c = len(cache_segs)
    meta_refs = refs[:n_c]
    out_refs = refs[n_c:n_c + len(segs)]
    cache_refs = refs[n_c + len(segs):2 * n_c + len(segs)]
    if n_c:
        stage, meta_stage, sem, meta_sem = refs[2 * n_c + len(segs):]
    i = pl.program_id(0)
    slot = i % 2
    batch = i // tiles_per_batch
    tile = i % tiles_per_batch

    def frames_copy(c, s):
        first = (N_META + tile * tm) * N_HEADS
        return pltpu.make_async_copy(
            stage.at[c, s], cache_refs[c].at[batch, pl.ds(first, tm * N_HEADS), :], sem.at[c, s])

    if n_c:
        @pl.when(i >= 2)
        def _():
            for c in range(n_c):
                frames_copy(c, slot).wait()

    h = h_ref[...]
    for out_idx, col0, width, rope in segs:
        cw = 256 if width % 256 == 0 else LANES
        for c in range(0, width, cw):
            z = _dot(h, w_ref[:, col0 + c:col0 + c + cw])
            for b in range(cw // LANES):
                zb = z[:, b * LANES:(b + 1) * LANES]
                if rope is not None:
                    zb = _rope_block(zb, tab_ref, rope[0], rope[1])
                out_refs[out_idx][:, c + b * LANES:c + (b + 1) * LANES] = zb
                if out_idx in cache_segs:
                    head = c // LANES + b
                    stage[cache_segs.index(out_idx), slot, pl.ds(head, tm, stride=N_HEADS), :] = zb

    for c in range(n_c):
        frames_copy(c, slot).start()

    if n_c:
        @pl.when(tile == 0)
        def _():
            for c in range(n_c):
                for head in range(N_HEADS):
                    meta_stage[c, pl.ds(head, N_META, stride=N_HEADS), :] = (
                        meta_refs[c][:, head * LANES:(head + 1) * LANES])
                copy = pltpu.make_async_copy(
                    meta_stage.at[c], cache_refs[c].at[batch, pl.ds(0, N_META * N_HEADS), :], meta_sem.at[c])
                copy.start()
                copy.wait()

        @pl.when(i == n_steps - 1)
        def _():
            for c in range(n_c):
                frames_copy(c, slot).wait()
                if n_steps > 1:
                    frames_copy(c, 1 - slot).wait()


def _proj(h, w, tab, segs, *, tm, name, cache=None):
    n, d = h.shape
    widths = [s[2] for s in segs]
    row = lambda i: (i, 0)
    const = lambda i: (0, 0)
    n_steps = n // tm
    seq, meta_rows = cache if cache is not None else (n, {})
    cache_segs = tuple(sorted(meta_rows))
    n_c = len(cache_segs)
    assert seq % tm == 0 and all(widths[s] == N_HEADS * LANES for s in cache_segs)
    assert tab.shape[0] % tm == 0 and n % tab.shape[0] == 0
    tab_tiles = tab.shape[0] // tm
    out_specs = [pl.BlockSpec((tm, wd), row) for wd in widths]
    out_shape = [jax.ShapeDtypeStruct((n, wd), F32) for wd in widths]
    out_specs += [pl.BlockSpec(memory_space=pl.ANY)] * n_c
    out_shape += [jax.ShapeDtypeStruct((n // seq, (N_META + seq) * N_HEADS, LANES), F32)] * n_c
    scratch = []
    if n_c:
        scratch = [pltpu.VMEM((n_c, 2, tm * N_HEADS, LANES), F32), pltpu.VMEM((n_c, N_META * N_HEADS, LANES), F32),
                   pltpu.SemaphoreType.DMA((n_c, 2)), pltpu.SemaphoreType.DMA((n_c,))]
    return pl.pallas_call(
        functools.partial(_proj_body, segs=tuple(segs), cache_segs=cache_segs, tm=tm,
                          tiles_per_batch=seq // tm, n_steps=n_steps),
        grid=(n_steps,),
        in_specs=[
            pl.BlockSpec((tm, d), row),
            pl.BlockSpec(w.shape, const),
            pl.BlockSpec((tm, tab.shape[1]), lambda i: (i % tab_tiles, 0)),
        ] + [pl.BlockSpec((N_META, widths[s]), const) for s in cache_segs],
        out_specs=out_specs,
        out_shape=out_shape,
        scratch_shapes=scratch,
        compiler_params=_params(),
        name=name,
    )(h, w, tab, *[meta_rows[s] for s in cache_segs])


def _rope_tables(pos):
    pos = pos.astype(F32)
    inv32 = ROPE_THETA ** (-jnp.arange(32, dtype=F32) / 32)
    inv64 = ROPE_THETA ** (-jnp.arange(64, dtype=F32) / 64)
    a32 = pos[:, None] * inv32[None, :]
    a64 = pos[:, None] * inv64[None, :]
    c32, s32 = jnp.cos(a32), jnp.sin(a32)
    c64, s64 = jnp.cos(a64), jnp.sin(a64)
    z32 = jnp.zeros_like(c32)
    z64 = jnp.zeros_like(c64)
    return jnp.concatenate(
        [c32, c32, c32, c32, -s32, z32, -s32, z32, z32, s32, z32, s32,
         c64, c64, -s64, s64,
         c32, c32, jnp.ones_like(c64), -s32, z32, z64, z32, s32, z64],
        axis=1)


TAB_D64, TAB_D128, TAB_MISC = 0, 3, 5


def _project_all(h, pos, w_a, w_b, w_ig, *, tm, cache_meta=None):
    tab = _rope_tables(pos)
    cache_a = cache_b = None
    if cache_meta is not None:
        seq, ak_m, av_m, bk_m, bv_m = cache_meta
        cache_a, cache_b = (seq, {1: ak_m, 2: av_m}), (seq, {1: bk_m, 2: bv_m})
    aq, ak, av, *rows_a = _proj(h, w_a, tab,
                                [(0, 0, A_WIDTH, ("d64", TAB_D64)), (1, A_WIDTH, A_WIDTH, ("d64", TAB_D64)),
                                 (2, 2 * A_WIDTH, A_WIDTH, None)], tm=tm, name="proj_a", cache=cache_a)
    bq, bk, bv, *rows_b = _proj(h, w_b, tab,
                                [(0, 0, B_WIDTH, ("d128", TAB_D128)), (1, B_WIDTH, B_WIDTH, ("d128", TAB_D128)),
                                 (2, 2 * B_WIDTH, B_WIDTH, None)], tm=tm, name="proj_b", cache=cache_b)
    iq, misc, gates = _proj(h, w_ig, tab,
                            [(0, 0, IQ_WIDTH, ("d64", TAB_D64)), (1, IQ_WIDTH, LANES, ("d64", TAB_MISC)),
                             (2, IQ_WIDTH + LANES, 2 * D_MODEL, None)], tm=tm, name="proj_ig")
    return (aq, ak, av, bq, bk, bv, iq, misc, gates), tuple(rows_a + rows_b)


def _lam(lam_ref):
    lp = lam_ref[...]
    s1 = jnp.sum(lp[0:1] * lp[1:2], axis=-1, keepdims=True)
    s2 = jnp.sum(lp[2:3] * lp[3:4], axis=-1, keepdims=True)
    return jnp.exp(s1) - jnp.exp(s2) + LAM_INIT


def _stack_diff_queries(q):
    lane = lax.broadcasted_iota(jnp.int32, q.shape, 1)
    q = q * (A_DIM ** -0.5 * LOG2E)
    q1 = jnp.where(lane < A_DIM, q, 0.0)
    q2 = jnp.where(lane >= A_DIM, q, 0.0)
    return jnp.concatenate([q1, q2], axis=0).astype(BF16)


def _diff_finish(acc, l, n, lam, g):
    o1 = acc[:n] * (1.0 / l[:n])
    o2 = acc[n:] * (1.0 / l[n:])
    o = o1 - lam * o2
    return _rms(o, g) * (1.0 - LAM_INIT)


def _diff_prompt_body(lam_ref, q_ref, k_ref, v_ref, km_ref, vm_ref, g_ref, o_ref, *, tq, tk, heads):
    qi = pl.program_id(2)
    lanes = [slice(u * LANES, (u + 1) * LANES) for u in range(heads)]
    qs = [_stack_diff_queries(q_ref[0, :, ln]) for ln in lanes]

    def update_head(q, state, k, v, visible):
        m, l, acc = state
        s = _dot_nt(q, k)
        if visible is not None:
            s = jnp.where(visible, s, NEG)
        m_new = jnp.maximum(m, jnp.max(s, axis=-1, keepdims=True))
        alpha = jnp.exp2(m - m_new)
        p = jnp.exp2(s - m_new)
        return m_new, alpha * l + jnp.sum(p, axis=-1, keepdims=True), alpha * acc + _dot(p.astype(BF16), v)

    def update(states, k, v, visible):
        return tuple(update_head(qs[u], states[u], k[:, lanes[u]].astype(BF16), v[:, lanes[u]].astype(BF16), visible)
                     for u in range(heads))

    def load(start, n):
        return k_ref[0, pl.ds(start, n), :], v_ref[0, pl.ds(start, n), :]

    col = lax.broadcasted_iota(jnp.int32, (2 * tq, META_PAD), 1)
    init = (jnp.full((2 * tq, 1), NEG, F32), jnp.zeros((2 * tq, 1), F32), jnp.zeros((2 * tq, LANES), F32))
    states = update((init,) * heads, km_ref[...], vm_ref[...], col < N_META)

    n_wide = (qi * tq) // tk
    states = lax.fori_loop(
        0, n_wide, lambda j, st: update(st, *load(pl.multiple_of(j * tk, tk), tk), None), states)
    states = lax.fori_loop(
        n_wide * (tk // tq), qi, lambda j, st: update(st, *load(pl.multiple_of(j * tq, tq), tq), None), states)
    r = lax.broadcasted_iota(jnp.int32, (2 * tq, tq), 0)
    c = lax.broadcasted_iota(jnp.int32, (2 * tq, tq), 1)
    r = jnp.where(r >= tq, r - tq, r)
    states = update(states, *load(pl.multiple_of(qi * tq, tq), tq), c // CHUNK <= r // CHUNK)
    lam = _lam(lam_ref)
    for u in range(heads):
        m, l, acc = states[u]
        o_ref[0, :, lanes[u]] = _diff_finish(acc, l, tq, lam, g_ref[...])


def _diff_prompt(aq, ak, av, akm, avm, lam4, g, *, tq):
    bsz, t, _ = aq.shape
    heads = 2
    tk = min(2 * tq, t)
    width = heads * LANES
    head_q = lambda b, h, i: (b, i, h)
    head_all = lambda b, h, i: (b, 0, h)
    head_meta = lambda b, h, i: (0, h)
    const = lambda b, h, i: (0, 0)
    return pl.pallas_call(
        functools.partial(_diff_prompt_body, tq=tq, tk=tk, heads=heads),
        grid=(bsz, N_HEADS // heads, t // tq),
        in_specs=[
            pl.BlockSpec(lam4.shape, const),
            pl.BlockSpec((1, tq, width), head_q),
            pl.BlockSpec((1, t, width), head_all),
            pl.BlockSpec((1, t, width), head_all),
            pl.BlockSpec((META_PAD, width), head_meta),
            pl.BlockSpec((META_PAD, width), head_meta),
            pl.BlockSpec((1, LANES), const),
        ],
        out_specs=pl.BlockSpec((1, tq, width), head_q),
        out_shape=jax.ShapeDtypeStruct(aq.shape, F32),
        compiler_params=_params(),
        name="diff_attn_prompt",
    )(lam4, aq, ak, av, akm, avm, g)


def _cache_head(cache_ref, h, first, n):
    return cache_ref[0, pl.ds(first * N_HEADS + h, n, stride=N_HEADS), :]


def _small_keys(cache_ref, new_ref, h, s_new):
    meta = _cache_head(cache_ref, h, 0, N_META)
    new = new_ref[:, h * LANES:(h + 1) * LANES]
    pad = jnp.zeros((META_PAD - N_META - s_new, LANES), F32)
    return jnp.concatenate([meta, new, pad], axis=0)


def _diff_sample_body(lam_ref, q_ref, kc_ref, vc_ref, kn_ref, vn_ref, g_ref, o_ref, *, s_new, past):
    lam = _lam(lam_ref)
    for h in range(N_HEADS):
        sl = slice(h * LANES, (h + 1) * LANES)
        qs = _stack_diff_queries(q_ref[:, sl])
        kp = _cache_head(kc_ref, h, N_META, past).astype(BF16)
        vp = _cache_head(vc_ref, h, N_META, past).astype(BF16)
        ks = _small_keys(kc_ref, kn_ref, h, s_new).astype(BF16)
        vs = _small_keys(vc_ref, vn_ref, h, s_new).astype(BF16)
        s_p = _dot_nt(qs, kp)
        s_s = _dot_nt(qs, ks)
        col = lax.broadcasted_iota(jnp.int32, s_s.shape, 1)
        s_s = jnp.where(col < N_META + s_new, s_s, NEG)
        m = jnp.maximum(jnp.max(s_p, axis=-1, keepdims=True), jnp.max(s_s, axis=-1, keepdims=True))
        p_p = jnp.exp2(s_p - m)
        p_s = jnp.exp2(s_s - m)
        l = jnp.sum(p_p, axis=-1, keepdims=True) + jnp.sum(p_s, axis=-1, keepdims=True)
        acc = _dot(p_p.astype(BF16), vp) + _dot(p_s.astype(BF16), vs)
        o_ref[:, sl] = _diff_finish(acc, l, s_new, lam, g_ref[...])


def _diff_sample(aq, ak, av, cache_k, cache_v, lam4, g, *, s_new):
    bsz, n_rows, _ = cache_k.shape
    width = aq.shape[1]
    past = n_rows // N_HEADS - N_META
    rows = lambda b: (b, 0)
    cache = lambda b: (b, 0, 0)
    const = lambda b: (0, 0)
    return pl.pallas_call(
        functools.partial(_diff_sample_body, s_new=s_new, past=past),
        grid=(bsz,),
        in_specs=[
            pl.BlockSpec(lam4.shape, const),
            pl.BlockSpec((s_new, width), rows),
            pl.BlockSpec((1, n_rows, LANES), cache),
            pl.BlockSpec((1, n_rows, LANES), cache),
            pl.BlockSpec((s_new, width), rows),
            pl.BlockSpec((s_new, width), rows),
            pl.BlockSpec((1, LANES), const),
        ],
        out_specs=pl.BlockSpec((s_new, width), rows),
        out_shape=jax.ShapeDtypeStruct(aq.shape, F32),
        compiler_params=_params(),
        name="diff_attn_sample",
    )(lam4, aq, cache_k, cache_v, ak, av, g)


def _stack_index_queries(iq, width):
    n = iq.shape[0]
    lane = lax.broadcasted_iota(jnp.int32, (n, LANES), 1)
    parts = []
    for h in range(IDX_HEADS):
        blk = iq[:, (h // 2) * LANES:(h // 2 + 1) * LANES]
        if h % 2:
            blk = pltpu.roll(blk, IDX_DIM, 1)
        blk = jnp.where(lane < IDX_DIM, blk, 0.0) * (IDX_DIM ** -0.5)
        parts.append(blk[:, :width])
    return jnp.concatenate(parts, axis=0).astype(BF16)


def _index_scores(qs, w, keys, n):
    rel = jnp.maximum(_dot_nt(qs, keys.astype(BF16)), 0.0)
    sc = rel[0:n] * w[:, IDX_DIM:IDX_DIM + 1]
    for h in range(1, IDX_HEADS):
        sc = sc + rel[h * n:(h + 1) * n] * w[:, IDX_DIM + h:IDX_DIM + h + 1]
    return sc * (IDX_HEADS ** -0.5)


def _sort_key(x):
    bits = lax.bitcast_convert_type(x, jnp.int32)
    key = bits ^ ((bits >> 31) & 0x7FFFFFFF)
    return jnp.where(key == -1, 0, key)


def _ind(pred):
    return jnp.where(pred, 1.0, 0.0)


def _lane_sum(x):
    acc = x[:, 0:LANES]
    for b in range(1, x.shape[1] // LANES):
        acc = acc + x[:, b * LANES:(b + 1) * LANES]
    return acc


def _select_threshold(count_fn, n_rows, n_top, idx_bits):
    def bit_step(i, thr):
        cand = thr ^ jnp.left_shift(jnp.int32(1), 31 - i)
        cnt = count_fn(lambda key, col, rows: _ind(key >= cand[rows]))
        return jnp.where(cnt >= n_top, cand, thr)

    thr = lax.fori_loop(0, 32, bit_step, jnp.full((n_rows, 1), INT_MIN, jnp.int32))
    need = n_top - count_fn(lambda key, col, rows: _ind(key > thr[rows]))
    excess = count_fn(lambda key, col, rows: _ind(key == thr[rows])) - need

    def tie_search(_):
        def idx_step(i, cut):
            cand = cut | jnp.left_shift(jnp.int32(1), idx_bits - 1 - i)
            cnt = count_fn(lambda key, col, rows: jnp.where(key == thr[rows], _ind(col < cand[rows]), 0.0))
            return jnp.where(cnt < need, cand, cut)
        return lax.fori_loop(0, idx_bits, idx_step, jnp.zeros((n_rows, 1), jnp.int32))

    no_ties = lambda _: jnp.full((n_rows, 1), 2 ** 30, jnp.int32)
    cut = lax.cond(jnp.max(excess) > 0.0, tie_search, no_ties, None)
    return thr, cut


def _select_bias(key, col, thr, cut):
    tie = jnp.where(key == thr, jnp.where(col <= cut, 0.0, NEG), NEG)
    bias = jnp.where(key > thr, 0.0, tie)
    return jnp.where(key > KEY_NEG_INF, bias, NEG)


def _split_key(key):
    hi = (key >> 16).astype(I16)
    lo = ((key & 0xFFFF) - 32768).astype(I16)
    return hi, lo


def _ind16(pred):
    return jnp.where(pred, jnp.ones(pred.shape, I16), jnp.zeros(pred.shape, I16))


def _dsa_index_body(iq_ref, wq_ref, ik_ref, ikm_ref, om_ref, of_ref, hi_m, lo_m, hi_f, lo_f, *, tq, tk,
                    n_tiles_all, n_top):
    qi = pl.program_id(1)
    qs = _stack_index_queries(iq_ref[0], LANES)
    w = wq_ref[0]

    sc = _index_scores(qs, w, ikm_ref[...], tq)
    lane = lax.broadcasted_iota(jnp.int32, (tq, LANES), 1)
    hi_m[...], lo_m[...] = _split_key(_sort_key(jnp.where(lane < N_META, sc, -jnp.inf)))

    row0 = qi * tq
    n_tiles = (row0 + tq - 1) // tk + 1
    rowc = (row0 + lax.broadcasted_iota(jnp.int32, (tq, tk), 0)) // CHUNK
    colt = lax.broadcasted_iota(jnp.int32, (tq, tk), 1)

    def score_tile(j, masked):
        start = pl.multiple_of(j * tk, tk)
        sc = _index_scores(qs, w, ik_ref[0, pl.ds(start, tk), :], tq)
        if masked:
            sc = jnp.where((start + colt) // CHUNK <= rowc, sc, -jnp.inf)
        hi_f[j], lo_f[j] = _split_key(_sort_key(sc))
        return 0

    n_open = row0 // tk
    lax.fori_loop(0, n_open, lambda j, _: score_tile(j, False), 0)
    lax.fori_loop(n_open, n_tiles, lambda j, _: score_tile(j, True), 0)

    def count(f):
        accs = []
        for r0 in range(0, tq, SEARCH_ROWS):
            rows = slice(r0, r0 + SEARCH_ROWS)
            accs.append(lax.fori_loop(
                0, n_tiles, lambda j, a: a + _lane_sum(f(hi_f[j, rows, :], lo_f[j, rows, :], rows)),
                f(hi_m[rows, :], lo_m[rows, :], rows)))
        return jnp.sum(jnp.concatenate(accs, axis=0).astype(F32), axis=-1, keepdims=True)

    def half_search(f_ge, need):
        def step(i, t):
            cand = t + jnp.left_shift(jnp.int32(1), 15 - i)
            c16 = cand.astype(I16)
            return jnp.where(count(f_ge(c16)) >= need, cand, t)
        return lax.fori_loop(0, 16, step, jnp.full((tq, 1), I16_MIN, jnp.int32)).astype(I16)

    th = half_search(lambda c: lambda hi, lo, rows: _ind16(hi >= c[rows]), n_top)
    need_lo = n_top - count(lambda hi, lo, rows: _ind16(hi > th[rows]))

    lo_m[...] = jnp.where(hi_m[...] == th, lo_m[...], I16_MIN)

    def tie_class_tile(j, _):
        lo_f[j] = jnp.where(hi_f[j] == th, lo_f[j], I16_MIN)
        return 0

    lax.fori_loop(0, n_tiles, tie_class_tile, 0)
    tl = half_search(lambda c: lambda hi, lo, rows: _ind16(lo >= c[rows]), need_lo)
    need = need_lo - count(lambda hi, lo, rows: _ind16(lo > tl[rows]))

    def bias(hi, lo, tri, before):
        one = jnp.ones(hi.shape, BF16)
        zero = jnp.zeros(hi.shape, BF16)
        above = jnp.where(hi > th, one, jnp.where(hi == th, jnp.where(lo > tl, one, zero), zero))
        tie = jnp.where(hi == th, jnp.where(lo == tl, one, zero), zero)
        tie = jnp.where(hi > HI_NEG_INF, tie, zero)
        rank = _dot(tie, tri)
        take = jnp.where(rank + before <= need, 1.0, 0.0).astype(BF16)
        chosen = above + tie * take
        return jnp.where(chosen > 0.5, zero, jnp.full(hi.shape, NEG, BF16)), before + rank[:, -1:]

    def upper_ones(n):
        r = lax.broadcasted_iota(jnp.int32, (n, n), 0)
        c = lax.broadcasted_iota(jnp.int32, (n, n), 1)
        return jnp.where(r <= c, 1.0, 0.0).astype(BF16)

    om_ref[0], ties = bias(hi_m[...], lo_m[...], upper_ones(META_PAD), jnp.zeros((tq, 1), F32))
    tri = upper_ones(tk)

    def write_tile(j, before):
        of_ref[0, j], after = bias(hi_f[j], lo_f[j], tri, before)
        return after

    lax.fori_loop(0, n_tiles, write_tile, ties)

    def fill_tile(j, _):
        of_ref[0, j] = jnp.full((tq, tk), NEG, BF16)
        return 0

    lax.fori_loop(n_tiles, n_tiles_all, fill_tile, 0)


def _dsa_index(iq, misc, misc_meta, *, tq, tk, n_top):
    bsz, t, _ = iq.shape
    nk = t // tk
    return pl.pallas_call(
        functools.partial(_dsa_index_body, tq=tq, tk=tk, n_tiles_all=nk, n_top=n_top),
        grid=(bsz, t // tq),
        in_specs=[
            pl.BlockSpec((1, tq, IQ_WIDTH), lambda b, i: (b, i, 0)),
            pl.BlockSpec((1, tq, LANES), lambda b, i: (b, i, 0)),
            pl.BlockSpec((1, t, LANES), lambda b, i: (b, 0, 0)),
            pl.BlockSpec((META_PAD, LANES), lambda b, i: (0, 0)),
        ],
        out_specs=[
            pl.BlockSpec((1, tq, LANES), lambda b, i: (b, i, 0)),
            pl.BlockSpec((1, nk, tq, tk), lambda b, i: (b, 0, i, 0)),
        ],
        out_shape=[
            jax.ShapeDtypeStruct((bsz, t, LANES), BF16),
            jax.ShapeDtypeStruct((bsz, nk, t, tk), BF16),
        ],
        scratch_shapes=[pltpu.VMEM((tq, LANES), I16), pltpu.VMEM((tq, LANES), I16),
                        pltpu.VMEM((nk, tq, tk), I16), pltpu.VMEM((nk, tq, tk), I16)],
        compiler_params=_params(),
        name="dsa_index",
    )(iq, misc, misc, misc_meta)


def _dsa_query(q):
    return (q * (B_DIM ** -0.5 * LOG2E)).astype(BF16)


def _masked_flash_update(q, k, v, bias, state):
    m, l, acc = state
    s = _dot_nt(q, k) + bias
    m_new = jnp.maximum(m, jnp.max(s, axis=-1, keepdims=True))
    alpha = jnp.exp2(m - m_new)
    p = jnp.exp2(s - m_new)
    return m_new, alpha * l + jnp.sum(p, axis=-1, keepdims=True), alpha * acc + _dot(p.astype(BF16), v)


def _flash_init(n):
    return jnp.full((n, 1), NEG, F32), jnp.zeros((n, 1), F32), jnp.zeros((n, LANES), F32)


def _dsa_attn_body(q_ref, k_ref, v_ref, km_ref, vm_ref, bm_ref, bf_ref, o_ref, *, tq, tb, group):
    qi = pl.program_id(2)
    q = _dsa_query(q_ref[0])

    def tiles(j0, n):
        start = pl.multiple_of(j0 * tb, tb)
        k = k_ref[0, pl.ds(start, n * tb), :].astype(BF16)
        v = v_ref[0, pl.ds(start, n * tb), :].astype(BF16)
        bias = jnp.concatenate([bf_ref[0, j0 + u].astype(F32) for u in range(n)], axis=1)
        return k, v, bias

    state = _masked_flash_update(q, km_ref[...].astype(BF16), vm_ref[...].astype(BF16), bm_ref[0].astype(F32),
                                 _flash_init(tq))
    n_tiles = (qi * tq + tq - 1) // tb + 1
    n_wide = n_tiles // group
    state = lax.fori_loop(
        0, n_wide, lambda g, st: _masked_flash_update(q, *tiles(g * group, group), st), state)
    m, l, acc = lax.fori_loop(
        n_wide * group, n_tiles, lambda j, st: _masked_flash_update(q, *tiles(j, 1), st), state)
    o_ref[0] = acc * (1.0 / l)


def _dsa_attn(bq, bk, bv, bkm, bvm, bias_m, bias_f, *, tq):
    bsz, t, _ = bq.shape
    nb, tb = bias_f.shape[1], bias_f.shape[3]
    group = max(1, min(2048, t) // tb)
    head_q = lambda b, h, i: (b, i, h)
    head_all = lambda b, h, i: (b, 0, h)
    head_meta = lambda b, h, i: (0, h)
    return pl.pallas_call(
        functools.partial(_dsa_attn_body, tq=tq, tb=tb, group=group),
        grid=(bsz, N_HEADS, t // tq),
        in_specs=[
            pl.BlockSpec((1, tq, LANES), head_q),
            pl.BlockSpec((1, t, LANES), head_all),
            pl.BlockSpec((1, t, LANES), head_all),
            pl.BlockSpec((META_PAD, LANES), head_meta),
            pl.BlockSpec((META_PAD, LANES), head_meta),
            pl.BlockSpec((1, tq, LANES), lambda b, h, i: (b, i, 0)),
            pl.BlockSpec((1, nb, tq, tb), lambda b, h, i: (b, 0, i, 0)),
        ],
        out_specs=pl.BlockSpec((1, tq, LANES), head_q),
        out_shape=jax.ShapeDtypeStruct(bq.shape, F32),
        compiler_params=_params(),
        name="dsa_attn_prompt",
    )(bq, bk, bv, bkm, bvm, bias_m, bias_f)


def _dsa_sample_body(iq_ref, wq_ref, kidx_ref, q_ref, kc_ref, vc_ref, kn_ref, vn_ref, o_ref, *,
                     s_new, past, n_top, idx_bits):
    qs = _stack_index_queries(iq_ref[...], IDX_DIM)
    w = wq_ref[...]
    ik_past = kidx_ref[0, N_META:N_META + past, :]
    ik_small = jnp.concatenate(
        [kidx_ref[0, 0:N_META, :], w[:, 0:IDX_DIM], jnp.zeros((META_PAD - N_META - s_new, IDX_DIM), F32)], axis=0)
    lane_s = lax.broadcasted_iota(jnp.int32, (s_new, META_PAD), 1)
    sc_s = _index_scores(qs, w, ik_small, s_new)
    key_s = _sort_key(jnp.where(lane_s < N_META + s_new, sc_s, -jnp.inf))
    key_p = _sort_key(_index_scores(qs, w, ik_past, s_new))
    col_s = jnp.where(lane_s < N_META, lane_s, lane_s + past)
    col_p = N_META + lax.broadcasted_iota(jnp.int32, (s_new, past), 1)

    every = slice(None)
    count_fn = lambda f: jnp.sum(
        f(key_s, col_s, every) + _lane_sum(f(key_p, col_p, every)), axis=-1, keepdims=True)
    thr, cut = _select_threshold(count_fn, s_new, n_top, idx_bits)
    bias_s = _select_bias(key_s, col_s, thr, cut)
    bias_p = _select_bias(key_p, col_p, thr, cut)

    for h in range(N_HEADS):
        sl = slice(h * LANES, (h + 1) * LANES)
        q = _dsa_query(q_ref[:, sl])
        state = _masked_flash_update(q, _small_keys(kc_ref, kn_ref, h, s_new).astype(BF16),
                                     _small_keys(vc_ref, vn_ref, h, s_new).astype(BF16), bias_s,
                                     _flash_init(s_new))
        m, l, acc = _masked_flash_update(q, _cache_head(kc_ref, h, N_META, past).astype(BF16),
                                         _cache_head(vc_ref, h, N_META, past).astype(BF16), bias_p, state)
        o_ref[:, sl] = acc * (1.0 / l)


def _dsa_sample(iq, misc, bq, bk, bv, cache_kidx, cache_k, cache_v, *, s_new, n_top):
    bsz, n_rows, _ = cache_k.shape
    width = bq.shape[1]
    n_cache = n_rows // N_HEADS
    past = n_cache - N_META
    idx_bits = (n_cache + s_new - 1).bit_length()
    rows = lambda b: (b, 0)
    cache = lambda b: (b, 0, 0)
    return pl.pallas_call(
        functools.partial(_dsa_sample_body, s_new=s_new, past=past, n_top=n_top, idx_bits=idx_bits),
        grid=(bsz,),
        in_specs=[
            pl.BlockSpec((s_new, IQ_WIDTH), rows),
            pl.BlockSpec((s_new, LANES), rows),
            pl.BlockSpec((1, n_cache, IDX_DIM), cache),
            pl.BlockSpec((s_new, width), rows),
            pl.BlockSpec((1, n_rows, LANES), cache),
            pl.BlockSpec((1, n_rows, LANES), cache),
            pl.BlockSpec((s_new, width), rows),
            pl.BlockSpec((s_new, width), rows),
        ],
        out_specs=pl.BlockSpec((s_new, width), rows),
        out_shape=jax.ShapeDtypeStruct(bq.shape, F32),
        compiler_params=_params(),
        name="dsa_sample",
    )(iq, misc, cache_kidx, bq, cache_k, cache_v, bk, bv)


def _merge_body(x_ref, oa_ref, ob_ref, g_ref, wa_ref, wb_ref, wo_ref, o_ref):
    ya = _dot(oa_ref[...].astype(BF16), wa_ref[...])
    yb = _dot(ob_ref[...].astype(BF16), wb_ref[...])
    merged = jax.nn.sigmoid(g_ref[:, :D_MODEL]) * ya + jax.nn.sigmoid(g_ref[:, D_MODEL:]) * yb
    o_ref[...] = x_ref[...] + _dot(merged.astype(BF16), wo_ref[...])


def _merge(x, oa, ob, gates, wa, wb, wo, *, tm):
    n, d = x.shape
    row = lambda i: (i, 0)
    const = lambda i: (0, 0)
    return pl.pallas_call(
        _merge_body,
        grid=(n // tm,),
        in_specs=[
            pl.BlockSpec((tm, d), row),
            pl.BlockSpec((tm, d), row),
            pl.BlockSpec((tm, d), row),
            pl.BlockSpec((tm, 2 * d), row),
            pl.BlockSpec(wa.shape, const),
            pl.BlockSpec(wb.shape, const),
            pl.BlockSpec(wo.shape, const),
        ],
        out_specs=pl.BlockSpec((tm, d), row),
        out_shape=jax.ShapeDtypeStruct((n, d), F32),
        compiler_params=_params(),
        name="merge",
    )(x, oa, ob, gates, wa, wb, wo)


def _row_tile(n, pref):
    tm = min(pref, n)
    assert n % tm == 0, (n, tm)
    return tm


def kernel(x_prompt, x_sample, cache_a_k, cache_a_v, cache_b_k, cache_b_v, cache_b_kidx, meta, g_ffn1, w1_gate,
           w1_up, w1_down, g_mix, w_in, lam_q1, lam_k1, lam_q2, lam_k2, a_subln, w_a, w_b, w_o, g_ffn2, w2_gate,
           w2_up, w2_down, g_final):
    bsz, seq, d = x_prompt.shape
    dec_b, dec_s, _ = x_sample.shape
    n_cache = cache_a_k.shape[2]
    past = n_cache - N_META
    assert d == D_MODEL and meta.shape == (N_META, D_MODEL)
    assert cache_a_k.shape[0] == 1, "single-layer step"
    assert past % LANES == 0 and (past % CHUNK) + dec_s <= CHUNK, "all cached and new keys visible to every new query"
    n_top_p = min(TOPK_MAX, seq // 4)
    n_top_s = min(TOPK_MAX, (past + dec_s) // 4)

    cast = lambda w: w[0].astype(BF16)
    w1g, w1u, w1d = cast(w1_gate), cast(w1_up), cast(w1_down)
    w2g, w2u, w2d = cast(w2_gate), cast(w2_up), cast(w2_down)
    wa, wb, wo = cast(w_a), cast(w_b), cast(w_o)
    win = w_in[0]
    c0 = 3 * A_WIDTH
    c1 = c0 + 3 * B_WIDTH
    c2 = c1 + IQ_WIDTH
    c3 = c2 + IDX_DIM + IDX_HEADS
    w_pa = win[:, :c0].astype(BF16)
    w_pb = win[:, c0:c1].astype(BF16)
    w_pig = jnp.concatenate(
        [win[:, c1:c3], jnp.zeros((d, LANES - IDX_DIM - IDX_HEADS), F32), win[:, c3:]], axis=1).astype(BF16)
    g1, gm, g2, gf = g_ffn1[0][None], g_mix[0][None], g_ffn2[0][None], g_final[None]
    lam4 = jnp.stack([lam_q1[0], lam_k1[0], lam_q2[0], lam_k2[0]]).astype(F32)
    gsub = a_subln[0][None]

    def front(x, pos, tm_ffn, tm_proj, cache_meta=None):
        x1, h = _ffn(x, g1, w1g, w1u, w1d, gm, tm=tm_ffn, emit_h=True)
        proj, cache_rows = _project_all(h, pos, w_pa, w_pb, w_pig, tm=tm_proj, cache_meta=cache_meta)
        return (x1,) + proj, cache_rows

    def back(x1, oa, ob, gates, tm, tm_ffn):
        x2 = _merge(x1, oa, ob, gates, wa, wb, wo, tm=tm)
        (y,) = _ffn(x2, g2, w2g, w2u, w2d, gf, tm=tm_ffn, emit_h=False)
        return y

    (_, _, ak_m, av_m, _, bk_m, bv_m, _, misc_m, _), _ = front(
        meta, jnp.arange(N_META, dtype=jnp.int32), N_META, N_META)
    pad_meta = lambda a: jnp.pad(a, ((0, META_PAD - N_META), (0, 0)))

    n_p = bsz * seq
    tm_p = _row_tile(seq, 512)
    tm_ffn_p = _row_tile(n_p, 1024)
    pos_p = N_META + jnp.arange(seq, dtype=jnp.int32)
    (x1_p, aq_p, ak_p, av_p, bq_p, bk_p, bv_p, iq_p, misc_p, gates_p), cache_rows_p = front(
        x_prompt.reshape(n_p, d), pos_p, tm_ffn_p, tm_p, cache_meta=(seq, ak_m, av_m, bk_m, bv_m))
    r3 = lambda a: a.reshape(bsz, seq, a.shape[-1])
    tq = _row_tile(seq, 512)
    oa_p = _diff_prompt(r3(aq_p), r3(ak_p), r3(av_p), pad_meta(ak_m), pad_meta(av_m), lam4, gsub, tq=tq)
    bias_m, bias_f = _dsa_index(r3(iq_p), r3(misc_p), pad_meta(misc_m), tq=tq, tk=tq,
                                n_top=n_top_p)
    ob_p = _dsa_attn(r3(bq_p), r3(bk_p), r3(bv_p), pad_meta(bk_m), pad_meta(bv_m), bias_m, bias_f, tq=tq)
    y_p = back(x1_p, oa_p.reshape(n_p, d), ob_p.reshape(n_p, d), gates_p, tm_p, tm_ffn_p).reshape(bsz, seq, d)

    n_s = dec_b * dec_s
    pos_s = N_META + past + jnp.tile(jnp.arange(dec_s, dtype=jnp.int32), dec_b)
    (x1_s, aq_s, ak_s, av_s, bq_s, bk_s, bv_s, iq_s, misc_s, gates_s), _ = front(
        x_sample.reshape(n_s, d), pos_s, n_s, n_s)
    c3d = lambda c: c[0].reshape(dec_b, n_cache * N_HEADS, LANES)
    oa_s = _diff_sample(aq_s, ak_s, av_s, c3d(cache_a_k), c3d(cache_a_v), lam4, gsub, s_new=dec_s)
    ob_s = _dsa_sample(iq_s, misc_s, bq_s, bk_s, bv_s, cache_b_kidx[0], c3d(cache_b_k), c3d(cache_b_v),
                       s_new=dec_s, n_top=n_top_s)
    y_s = back(x1_s, oa_s, ob_s, gates_s, n_s, n_s).reshape(dec_b, dec_s, d)

    ha, hb, hi = (A_HEADS, 2 * A_DIM), (B_HEADS, B_DIM), (IDX_DIM,)
    rows_p = lambda a, heads: a.reshape((1, bsz, N_META + seq) + heads)
    rows_s = lambda a, heads: a.reshape((1, dec_b, dec_s) + heads)
    kidx_p = jnp.concatenate(
        [jnp.broadcast_to(misc_m[None, :, :IDX_DIM], (bsz, N_META, IDX_DIM)), r3(misc_p)[:, :, :IDX_DIM]], axis=1)
    ak_r, av_r, bk_r, bv_r = cache_rows_p
    return (
        y_p, y_s,
        rows_p(ak_r, ha), rows_p(av_r, ha), rows_p(bk_r, hb), rows_p(bv_r, hb), kidx_p[None],
        rows_s(ak_s, ha), rows_s(av_s, ha), rows_s(bk_s, hb), rows_s(bv_s, hb), rows_s(misc_s[:, :IDX_DIM], hi),
    )
```

```python
import functools

import jax
import jax.numpy as jnp
from jax import lax
from jax.experimental import pallas as pl
from jax.experimental.pallas import tpu as pltpu

D_MODEL = 1024
CHUNK = 64
N_META = 16
ROPE_THETA = 10000.0
EPS = 1e-6
A_HEADS = 8
A_DIM = 64
B_HEADS = 8
B_DIM = 128
IDX_HEADS = 8
IDX_DIM = 64
TOPK_MAX = 256
LAM_INIT = 0.2

LANES = 128
N_HEADS = 8
A_WIDTH = A_HEADS * 2 * A_DIM
B_WIDTH = B_HEADS * B_DIM
IQ_WIDTH = IDX_HEADS * IDX_DIM
META_PAD = LANES
SEARCH_ROWS = 128

VMEM_LIMIT = 56 * 1024 * 1024
NEG = -1e30
INT_MIN = -(2 ** 31)
KEY_NEG_INF = -2139095041
HI_NEG_INF = KEY_NEG_INF >> 16
I16_MIN = -(2 ** 15)
LOG2E = 1.4426950408889634

F32 = jnp.float32
BF16 = jnp.bfloat16
I16 = jnp.int16

_NT = (((1,), (1,)), ((), ()))


def _params():
    return pltpu.CompilerParams(vmem_limit_bytes=VMEM_LIMIT)


def _dot(a, b):
    return jnp.dot(a, b, preferred_element_type=F32)


def _dot_nt(a, b):
    return lax.dot_general(a, b, _NT, preferred_element_type=F32)


def _rms(x, g):
    ms = jnp.mean(x * x, axis=-1, keepdims=True)
    return x * lax.rsqrt(ms + EPS) * g


def _ffn_body(x_ref, g_ref, wg_ref, wu_ref, wd_ref, gn_ref, *rest, n_ff, emit_h):
    if emit_h:
        o_ref, hn_ref, h_scr, acc_scr = rest
    else:
        o_ref, h_scr, acc_scr = rest
    j = pl.program_id(1)

    @pl.when(j == 0)
    def _():
        h_scr[...] = _rms(x_ref[...], g_ref[...]).astype(BF16)
        acc_scr[...] = jnp.zeros_like(acc_scr)

    h = h_scr[...]
    gate = _dot(h, wg_ref[...])
    up = _dot(h, wu_ref[...])
    act = gate * jax.nn.sigmoid(gate) * up
    acc_scr[...] += _dot(act.astype(BF16), wd_ref[...])

    @pl.when(j == n_ff - 1)
    def _():
        y = x_ref[...] + 0.5 * acc_scr[...]
        if emit_h:
            o_ref[...] = y
            hn_ref[...] = _rms(y, gn_ref[...]).astype(BF16)
        else:
            o_ref[...] = _rms(y, gn_ref[...])


def _ffn(x, g, wg, wu, wd, gn, *, tm, emit_h):
    n, d = x.shape
    dff = wg.shape[1]
    tf = 256 if dff % 256 == 0 else LANES
    n_ff = dff // tf
    row = lambda i, j: (i, 0)
    const = lambda i, j: (0, 0)
    out_shape = [jax.ShapeDtypeStruct((n, d), F32)]
    out_specs = [pl.BlockSpec((tm, d), row)]
    if emit_h:
        out_shape.append(jax.ShapeDtypeStruct((n, d), BF16))
        out_specs.append(pl.BlockSpec((tm, d), row))
    return pl.pallas_call(
        functools.partial(_ffn_body, n_ff=n_ff, emit_h=emit_h),
        grid=(n // tm, n_ff),
        in_specs=[
            pl.BlockSpec((tm, d), row),
            pl.BlockSpec((1, d), const),
            pl.BlockSpec((d, tf), lambda i, j: (0, j)),
            pl.BlockSpec((d, tf), lambda i, j: (0, j)),
            pl.BlockSpec((tf, d), lambda i, j: (j, 0)),
            pl.BlockSpec((1, d), const),
        ],
        out_specs=out_specs,
        out_shape=out_shape,
        scratch_shapes=[pltpu.VMEM((tm, d), BF16), pltpu.VMEM((tm, d), F32)],
        compiler_params=_params(),
        name="ffn_h" if emit_h else "ffn_final",
    )(x, g, wg, wu, wd, gn)


def _rope_block(z, tab_ref, kind, t0):
    cos = tab_ref[:, t0 * LANES:(t0 + 1) * LANES]
    s_a = tab_ref[:, (t0 + 1) * LANES:(t0 + 2) * LANES]
    if kind == "d128":
        return z * cos + pltpu.roll(z, 64, 1) * s_a
    s_b = tab_ref[:, (t0 + 2) * LANES:(t0 + 3) * LANES]
    return z * cos + pltpu.roll(z, 96, 1) * s_a + pltpu.roll(z, 32, 1) * s_b


def _proj_body(h_ref, w_ref, tab_ref, *refs, segs, cache_segs, tm, tiles_per_batch, n_steps):
    n_c = len(cache_segs)
    meta_refs = refs[:n_c]
    out_refs = refs[n_c:n_c + len(segs)]
    cache_refs = refs[n_c + len(segs):2 * n_c + len(segs)]
    if n_c:
        stage, meta_stage, sem, meta_sem = refs[2 * n_c + len(segs):]
    i = pl.program_id(0)
    slot = i % 2
    batch = i // tiles_per_batch
    tile = i % tiles_per_batch

    def frames_copy(c, s):
        first = (N_META + tile * tm) * N_HEADS
        return pltpu.make_async_copy(
            stage.at[c, s], cache_refs[c].at[batch, pl.ds(first, tm * N_HEADS), :], sem.at[c, s])

    if n_c:
        @pl.when(i >= 2)
        def _():
            for c in range(n_c):
                frames_copy(c, slot).wait()

    h = h_ref[...]
    for out_idx, col0, width, rope in segs:
        cw = 256 if width % 256 == 0 else LANES
        for c in range(0, width, cw):
            z = _dot(h, w_ref[:, col0 + c:col0 + c + cw])
            for b in range(cw // LANES):
                zb = z[:, b * LANES:(b + 1) * LANES]
                if rope is not None:
                    zb = _rope_block(zb, tab_ref, rope[0], rope[1])
                out_refs[out_idx][:, c + b * LANES:c + (b + 1) * LANES] = zb.astype(out_refs[out_idx].dtype)
                if out_idx in cache_segs:
                    head = c // LANES + b
                    stage[cache_segs.index(out_idx), slot, pl.ds(head, tm, stride=N_HEADS), :] = zb

    for c in range(n_c):
        frames_copy(c, slot).start()

    if n_c:
        @pl.when(tile == 0)
        def _():
            for c in range(n_c):
                for head in range(N_HEADS):
                    meta_stage[c, pl.ds(head, N_META, stride=N_HEADS), :] = (
                        meta_refs[c][:, head * LANES:(head + 1) * LANES])
                copy = pltpu.make_async_copy(
                    meta_stage.at[c], cache_refs[c].at[batch, pl.ds(0, N_META * N_HEADS), :], meta_sem.at[c])
                copy.start()
                copy.wait()

        @pl.when(i == n_steps - 1)
        def _():
            for c in range(n_c):
                frames_copy(c, slot).wait()
                if n_steps > 1:
                    frames_copy(c, 1 - slot).wait()


def _proj(h, w, tab, segs, *, tm, name, cache=None):
    n, d = h.shape
    widths = [s[2] for s in segs]
    row = lambda i: (i, 0)
    const = lambda i: (0, 0)
    n_steps = n // tm
    seq, meta_rows = cache if cache is not None else (n, {})
    cache_segs = tuple(sorted(meta_rows))
    n_c = len(cache_segs)
    assert seq % tm == 0 and all(widths[s] == N_HEADS * LANES for s in cache_segs)
    assert tab.shape[0] % tm == 0 and n % tab.shape[0] == 0
    tab_tiles = tab.shape[0] // tm
    out_specs = [pl.BlockSpec((tm, wd), row) for wd in widths]
    out_shape = [jax.ShapeDtypeStruct((n, wd), BF16 if s in cache_segs else F32) for s, wd in enumerate(widths)]
    out_specs += [pl.BlockSpec(memory_space=pl.ANY)] * n_c
    out_shape += [jax.ShapeDtypeStruct((n // seq, (N_META + seq) * N_HEADS, LANES), F32)] * n_c
    scratch = []
    if n_c:
        scratch = [pltpu.VMEM((n_c, 2, tm * N_HEADS, LANES), F32), pltpu.VMEM((n_c, N_META * N_HEADS, LANES), F32),
                   pltpu.SemaphoreType.DMA((n_c, 2)), pltpu.SemaphoreType.DMA((n_c,))]
    return pl.pallas_call(
        functools.partial(_proj_body, segs=tuple(segs), cache_segs=cache_segs, tm=tm,
                          tiles_per_batch=seq // tm, n_steps=n_steps),
        grid=(n_steps,),
        in_specs=[
            pl.BlockSpec((tm, d), row),
            pl.BlockSpec(w.shape, const),
            pl.BlockSpec((tm, tab.shape[1]), lambda i: (i % tab_tiles, 0)),
        ] + [pl.BlockSpec((N_META, widths[s]), const) for s in cache_segs],
        out_specs=out_specs,
        out_shape=out_shape,
        scratch_shapes=scratch,
        compiler_params=_params(),
        name=name,
    )(h, w, tab, *[meta_rows[s] for s in cache_segs])


def _rope_tables(pos):
    pos = pos.astype(F32)
    inv32 = ROPE_THETA ** (-jnp.arange(32, dtype=F32) / 32)
    inv64 = ROPE_THETA ** (-jnp.arange(64, dtype=F32) / 64)
    a32 = pos[:, None] * inv32[None, :]
    a64 = pos[:, None] * inv64[None, :]
    c32, s32 = jnp.cos(a32), jnp.sin(a32)
    c64, s64 = jnp.cos(a64), jnp.sin(a64)
    z32 = jnp.zeros_like(c32)
    z64 = jnp.zeros_like(c64)
    return jnp.concatenate(
        [c32, c32, c32, c32, -s32, z32, -s32, z32, z32, s32, z32, s32,
         c64, c64, -s64, s64,
         c32, c32, jnp.ones_like(c64), -s32, z32, z64, z32, s32, z64],
        axis=1)


TAB_D64, TAB_D128, TAB_MISC = 0, 3, 5


def _project_all(h, pos, w_a, w_b, w_ig, *, tm, cache_meta=None):
    tab = _rope_tables(pos)
    cache_a = cache_b = None
    if cache_meta is not None:
        seq, ak_m, av_m, bk_m, bv_m = cache_meta
        cache_a, cache_b = (seq, {1: ak_m, 2: av_m}), (seq, {1: bk_m, 2: bv_m})
    aq, ak, av, *rows_a = _proj(h, w_a, tab,
                                [(0, 0, A_WIDTH, ("d64", TAB_D64)), (1, A_WIDTH, A_WIDTH, ("d64", TAB_D64)),
                                 (2, 2 * A_WIDTH, A_WIDTH, None)], tm=tm, name="proj_a", cache=cache_a)
    bq, bk, bv, *rows_b = _proj(h, w_b, tab,
                                [(0, 0, B_WIDTH, ("d128", TAB_D128)), (1, B_WIDTH, B_WIDTH, ("d128", TAB_D128)),
                                 (2, 2 * B_WIDTH, B_WIDTH, None)], tm=tm, name="proj_b", cache=cache_b)
    iq, misc, gates = _proj(h, w_ig, tab,
                            [(0, 0, IQ_WIDTH, ("d64", TAB_D64)), (1, IQ_WIDTH, LANES, ("d64", TAB_MISC)),
                             (2, IQ_WIDTH + LANES, 2 * D_MODEL, None)], tm=tm, name="proj_ig")
    return (aq, ak, av, bq, bk, bv, iq, misc, gates), tuple(rows_a + rows_b)


def _lam(lam_ref):
    lp = lam_ref[...]
    s1 = jnp.sum(lp[0:1] * lp[1:2], axis=-1, keepdims=True)
    s2 = jnp.sum(lp[2:3] * lp[3:4], axis=-1, keepdims=True)
    return jnp.exp(s1) - jnp.exp(s2) + LAM_INIT


def _stack_diff_queries(q):
    lane = lax.broadcasted_iota(jnp.int32, q.shape, 1)
    q = q * (A_DIM ** -0.5 * LOG2E)
    q1 = jnp.where(lane < A_DIM, q, 0.0)
    q2 = jnp.where(lane >= A_DIM, q, 0.0)
    return jnp.concatenate([q1, q2], axis=0).astype(BF16)


def _diff_finish(acc, l, n, lam, g):
    o1 = acc[:n] * (1.0 / l[:n])
    o2 = acc[n:] * (1.0 / l[n:])
    o = o1 - lam * o2
    return _rms(o, g) * (1.0 - LAM_INIT)


def _diff_prompt_body(lam_ref, q_ref, k_ref, v_ref, km_ref, vm_ref, g_ref, o_ref, *, tq, tk, heads):
    qi = pl.program_id(2)
    lanes = [slice(u * LANES, (u + 1) * LANES) for u in range(heads)]
    qs = [_stack_diff_queries(q_ref[0, :, ln]) for ln in lanes]

    def update_head(q, state, k, v, visible):
        m, l, acc = state
        s = _dot_nt(q, k)
        if visible is not None:
            s = jnp.where(visible, s, NEG)
        m_new = jnp.maximum(m, jnp.max(s, axis=-1, keepdims=True))
        alpha = jnp.exp2(m - m_new)
        p = jnp.exp2(s - m_new)
        return m_new, alpha * l + jnp.sum(p, axis=-1, keepdims=True), alpha * acc + _dot(p.astype(BF16), v)

    def update(states, k, v, visible):
        return tuple(update_head(qs[u], states[u], k[:, lanes[u]].astype(BF16), v[:, lanes[u]].astype(BF16), visible)
                     for u in range(heads))

    def load(start, n):
        return k_ref[0, pl.ds(start, n), :], v_ref[0, pl.ds(start, n), :]

    col = lax.broadcasted_iota(jnp.int32, (2 * tq, META_PAD), 1)
    init = (jnp.full((2 * tq, 1), NEG, F32), jnp.zeros((2 * tq, 1), F32), jnp.zeros((2 * tq, LANES), F32))
    states = update((init,) * heads, km_ref[...], vm_ref[...], col < N_META)

    n_wide = (qi * tq) // tk
    states = lax.fori_loop(
        0, n_wide, lambda j, st: update(st, *load(pl.multiple_of(j * tk, tk), tk), None), states)
    states = lax.fori_loop(
        n_wide * (tk // tq), qi, lambda j, st: update(st, *load(pl.multiple_of(j * tq, tq), tq), None), states)
    r = lax.broadcasted_iota(jnp.int32, (2 * tq, tq), 0)
    c = lax.broadcasted_iota(jnp.int32, (2 * tq, tq), 1)
    r = jnp.where(r >= tq, r - tq, r)
    states = update(states, *load(pl.multiple_of(qi * tq, tq), tq), c // CHUNK <= r // CHUNK)
    lam = _lam(lam_ref)
    for u in range(heads):
        m, l, acc = states[u]
        o_ref[0, :, lanes[u]] = _diff_finish(acc, l, tq, lam, g_ref[...])


def _diff_prompt(aq, ak, av, akm, avm, lam4, g, *, tq):
    bsz, t, _ = aq.shape
    heads = 2
    tk = min(4 * tq, t)
    width = heads * LANES
    head_q = lambda b, h, i: (b, i, h)
    head_all = lambda b, h, i: (b, 0, h)
    head_meta = lambda b, h, i: (0, h)
    const = lambda b, h, i: (0, 0)
    return pl.pallas_call(
        functools.partial(_diff_prompt_body, tq=tq, tk=tk, heads=heads),
        grid=(bsz, N_HEADS // heads, t // tq),
        in_specs=[
            pl.BlockSpec(lam4.shape, const),
            pl.BlockSpec((1, tq, width), head_q),
            pl.BlockSpec((1, t, width), head_all),
            pl.BlockSpec((1, t, width), head_all),
            pl.BlockSpec((META_PAD, width), head_meta),
            pl.BlockSpec((META_PAD, width), head_meta),
            pl.BlockSpec((1, LANES), const),
        ],
        out_specs=pl.BlockSpec((1, tq, width), head_q),
        out_shape=jax.ShapeDtypeStruct(aq.shape, F32),
        compiler_params=_params(),
        name="diff_attn_prompt",
    )(lam4, aq, ak, av, akm, avm, g)


def _cache_head(cache_ref, h, first, n):
    return cache_ref[0, pl.ds(first * N_HEADS + h, n, stride=N_HEADS), :]


def _small_keys(cache_ref, new_ref, h, s_new):
    meta = _cache_head(cache_ref, h, 0, N_META)
    new = new_ref[:, h * LANES:(h + 1) * LANES]
    pad = jnp.zeros((META_PAD - N_META - s_new, LANES), F32)
    return jnp.concatenate([meta, new, pad], axis=0)


def _diff_sample_body(lam_ref, q_ref, kc_ref, vc_ref, kn_ref, vn_ref, g_ref, o_ref, *, s_new, past):
    lam = _lam(lam_ref)
    for h in range(N_HEADS):
        sl = slice(h * LANES, (h + 1) * LANES)
        qs = _stack_diff_queries(q_ref[:, sl])
        kp = _cache_head(kc_ref, h, N_META, past).astype(BF16)
        vp = _cache_head(vc_ref, h, N_META, past).astype(BF16)
        ks = _small_keys(kc_ref, kn_ref, h, s_new).astype(BF16)
        vs = _small_keys(vc_ref, vn_ref, h, s_new).astype(BF16)
        s_p = _dot_nt(qs, kp)
        s_s = _dot_nt(qs, ks)
        col = lax.broadcasted_iota(jnp.int32, s_s.shape, 1)
        s_s = jnp.where(col < N_META + s_new, s_s, NEG)
        m = jnp.maximum(jnp.max(s_p, axis=-1, keepdims=True), jnp.max(s_s, axis=-1, keepdims=True))
        p_p = jnp.exp2(s_p - m)
        p_s = jnp.exp2(s_s - m)
        l = jnp.sum(p_p, axis=-1, keepdims=True) + jnp.sum(p_s, axis=-1, keepdims=True)
        acc = _dot(p_p.astype(BF16), vp) + _dot(p_s.astype(BF16), vs)
        o_ref[:, sl] = _diff_finish(acc, l, s_new, lam, g_ref[...])


def _diff_sample(aq, ak, av, cache_k, cache_v, lam4, g, *, s_new):
    bsz, n_rows, _ = cache_k.shape
    width = aq.shape[1]
    past = n_rows // N_HEADS - N_META
    rows = lambda b: (b, 0)
    cache = lambda b: (b, 0, 0)
    const = lambda b: (0, 0)
    return pl.pallas_call(
        functools.partial(_diff_sample_body, s_new=s_new, past=past),
        grid=(bsz,),
        in_specs=[
            pl.BlockSpec(lam4.shape, const),
            pl.BlockSpec((s_new, width), rows),
            pl.BlockSpec((1, n_rows, LANES), cache),
            pl.BlockSpec((1, n_rows, LANES), cache),
            pl.BlockSpec((s_new, width), rows),
            pl.BlockSpec((s_new, width), rows),
            pl.BlockSpec((1, LANES), const),
        ],
        out_specs=pl.BlockSpec((s_new, width), rows),
        out_shape=jax.ShapeDtypeStruct(aq.shape, F32),
        compiler_params=_params(),
        name="diff_attn_sample",
    )(lam4, aq, cache_k, cache_v, ak, av, g)


def _stack_index_queries(iq, width):
    n = iq.shape[0]
    lane = lax.broadcasted_iota(jnp.int32, (n, LANES), 1)
    parts = []
    for h in range(IDX_HEADS):
        blk = iq[:, (h // 2) * LANES:(h // 2 + 1) * LANES]
        if h % 2:
            blk = pltpu.roll(blk, IDX_DIM, 1)
        blk = jnp.where(lane < IDX_DIM, blk, 0.0) * (IDX_DIM ** -0.5)
        parts.append(blk[:, :width])
    return jnp.concatenate(parts, axis=0).astype(BF16)


def _index_scores(qs, w, keys, n):
    rel = jnp.maximum(_dot_nt(qs, keys.astype(BF16)), 0.0)
    sc = rel[0:n] * w[:, IDX_DIM:IDX_DIM + 1]
    for h in range(1, IDX_HEADS):
        sc = sc + rel[h * n:(h + 1) * n] * w[:, IDX_DIM + h:IDX_DIM + h + 1]
    return sc * (IDX_HEADS ** -0.5)


def _sort_key(x):
    bits = lax.bitcast_convert_type(x, jnp.int32)
    key = bits ^ ((bits >> 31) & 0x7FFFFFFF)
    return jnp.where(key == -1, 0, key)


def _ind(pred):
    return jnp.where(pred, 1.0, 0.0)


def _lane_sum(x):
    acc = x[:, 0:LANES]
    for b in range(1, x.shape[1] // LANES):
        acc = acc + x[:, b * LANES:(b + 1) * LANES]
    return acc


def _select_threshold(count_fn, n_rows, n_top, idx_bits):
    def bit_step(i, thr):
        cand = thr ^ jnp.left_shift(jnp.int32(1), 31 - i)
        cnt = count_fn(lambda key, col, rows: _ind(key >= cand[rows]))
        return jnp.where(cnt >= n_top, cand, thr)

    thr = lax.fori_loop(0, 32, bit_step, jnp.full((n_rows, 1), INT_MIN, jnp.int32))
    need = n_top - count_fn(lambda key, col, rows: _ind(key > thr[rows]))
    excess = count_fn(lambda key, col, rows: _ind(key == thr[rows])) - need

    def tie_search(_):
        def idx_step(i, cut):
            cand = cut | jnp.left_shift(jnp.int32(1), idx_bits - 1 - i)
            cnt = count_fn(lambda key, col, rows: jnp.where(key == thr[rows], _ind(col < cand[rows]), 0.0))
            return jnp.where(cnt < need, cand, cut)
        return lax.fori_loop(0, idx_bits, idx_step, jnp.zeros((n_rows, 1), jnp.int32))

    no_ties = lambda _: jnp.full((n_rows, 1), 2 ** 30, jnp.int32)
    cut = lax.cond(jnp.max(excess) > 0.0, tie_search, no_ties, None)
    return thr, cut


def _select_bias(key, col, thr, cut):
    tie = jnp.where(key == thr, jnp.where(col <= cut, 0.0, NEG), NEG)
    bias = jnp.where(key > thr, 0.0, tie)
    return jnp.where(key > KEY_NEG_INF, bias, NEG)


def _split_key(key):
    hi = (key >> 16).astype(I16)
    lo = ((key & 0xFFFF) - 32768).astype(I16)
    return hi, lo


def _ind16(pred):
    return jnp.where(pred, jnp.ones(pred.shape, I16), jnp.zeros(pred.shape, I16))


def _dsa_index_body(iq_ref, wq_ref, ik_ref, ikm_ref, om_ref, of_ref, hi_m, lo_m, hi_f, lo_f, *, tq, tk,
                    n_tiles_all, n_top):
    qi = pl.program_id(1)
    qs = _stack_index_queries(iq_ref[0], LANES)
    w = wq_ref[0]

    sc = _index_scores(qs, w, ikm_ref[...], tq)
    lane = lax.broadcasted_iota(jnp.int32, (tq, LANES), 1)
    hi_m[...], lo_m[...] = _split_key(_sort_key(jnp.where(lane < N_META, sc, -jnp.inf)))

    row0 = qi * tq
    n_tiles = (row0 + tq - 1) // tk + 1
    rowc = (row0 + lax.broadcasted_iota(jnp.int32, (tq, tk), 0)) // CHUNK
    colt = lax.broadcasted_iota(jnp.int32, (tq, tk), 1)

    def score_tile(j, masked):
        start = pl.multiple_of(j * tk, tk)
        sc = _index_scores(qs, w, ik_ref[0, pl.ds(start, tk), :], tq)
        if masked:
            sc = jnp.where((start + colt) // CHUNK <= rowc, sc, -jnp.inf)
        hi_f[j], lo_f[j] = _split_key(_sort_key(sc))
        return 0

    n_open = row0 // tk
    lax.fori_loop(0, n_open, lambda j, _: score_tile(j, False), 0)
    lax.fori_loop(n_open, n_tiles, lambda j, _: score_tile(j, True), 0)

    def count(f):
        accs = []
        for r0 in range(0, tq, SEARCH_ROWS):
            rows = slice(r0, r0 + SEARCH_ROWS)
            accs.append(lax.fori_loop(
                0, n_tiles, lambda j, a: a + _lane_sum(f(hi_f[j, rows, :], lo_f[j, rows, :], rows)),
                f(hi_m[rows, :], lo_m[rows, :], rows)))
        return jnp.sum(jnp.concatenate(accs, axis=0).astype(F32), axis=-1, keepdims=True)

    def half_search(f_ge, need):
        def step(i, t):
            cand = t + jnp.left_shift(jnp.int32(1), 15 - i)
            c16 = cand.astype(I16)
            return jnp.where(count(f_ge(c16)) >= need, cand, t)
        return lax.fori_loop(0, 16, step, jnp.full((tq, 1), I16_MIN, jnp.int32)).astype(I16)

    th = half_search(lambda c: lambda hi, lo, rows: _ind16(hi >= c[rows]), n_top)
    need_lo = n_top - count(lambda hi, lo, rows: _ind16(hi > th[rows]))

    lo_m[...] = jnp.where(hi_m[...] == th, lo_m[...], I16_MIN)

    def tie_class_tile(j, _):
        lo_f[j] = jnp.where(hi_f[j] == th, lo_f[j], I16_MIN)
        return 0

    lax.fori_loop(0, n_tiles, tie_class_tile, 0)
    tl = half_search(lambda c: lambda hi, lo, rows: _ind16(lo >= c[rows]), need_lo)
    need = need_lo - count(lambda hi, lo, rows: _ind16(lo > tl[rows]))

    def bias(hi, lo, tri, before):
        one = jnp.ones(hi.shape, BF16)
        zero = jnp.zeros(hi.shape, BF16)
        above = jnp.where(hi > th, one, jnp.where(hi == th, jnp.where(lo > tl, one, zero), zero))
        tie = jnp.where(hi == th, jnp.where(lo == tl, one, zero), zero)
        tie = jnp.where(hi > HI_NEG_INF, tie, zero)
        rank = _dot(tie, tri)
        take = jnp.where(rank + before <= need, 1.0, 0.0).astype(BF16)
        chosen = above + tie * take
        return jnp.where(chosen > 0.5, zero, jnp.full(hi.shape, NEG, BF16)), before + rank[:, -1:]

    def upper_ones(n):
        r = lax.broadcasted_iota(jnp.int32, (n, n), 0)
        c = lax.broadcasted_iota(jnp.int32, (n, n), 1)
        return jnp.where(r <= c, 1.0, 0.0).astype(BF16)

    om_ref[0], ties = bias(hi_m[...], lo_m[...], upper_ones(META_PAD), jnp.zeros((tq, 1), F32))
    tri = upper_ones(tk)

    def write_tile(j, before):
        of_ref[0, j], after = bias(hi_f[j], lo_f[j], tri, before)
        return after

    lax.fori_loop(0, n_tiles, write_tile, ties)

    def fill_tile(j, _):
        of_ref[0, j] = jnp.full((tq, tk), NEG, BF16)
        return 0

    lax.fori_loop(n_tiles, n_tiles_all, fill_tile, 0)


def _dsa_index(iq, misc, misc_meta, *, tq, tk, n_top):
    bsz, t, _ = iq.shape
    nk = t // tk
    return pl.pallas_call(
        functools.partial(_dsa_index_body, tq=tq, tk=tk, n_tiles_all=nk, n_top=n_top),
        grid=(bsz, t // tq),
        in_specs=[
            pl.BlockSpec((1, tq, IQ_WIDTH), lambda b, i: (b, i, 0)),
            pl.BlockSpec((1, tq, LANES), lambda b, i: (b, i, 0)),
            pl.BlockSpec((1, t, LANES), lambda b, i: (b, 0, 0)),
            pl.BlockSpec((META_PAD, LANES), lambda b, i: (0, 0)),
        ],
        out_specs=[
            pl.BlockSpec((1, tq, LANES), lambda b, i: (b, i, 0)),
            pl.BlockSpec((1, nk, tq, tk), lambda b, i: (b, 0, i, 0)),
        ],
        out_shape=[
            jax.ShapeDtypeStruct((bsz, t, LANES), BF16),
            jax.ShapeDtypeStruct((bsz, nk, t, tk), BF16),
        ],
        scratch_shapes=[pltpu.VMEM((tq, LANES), I16), pltpu.VMEM((tq, LANES), I16),
                        pltpu.VMEM((nk, tq, tk), I16), pltpu.VMEM((nk, tq, tk), I16)],
        compiler_params=_params(),
        name="dsa_index",
    )(iq, misc, misc, misc_meta)


def _dsa_query(q):
    return (q * (B_DIM ** -0.5 * LOG2E)).astype(BF16)


def _masked_flash_update(q, k, v, bias, state):
    m, l, acc = state
    s = _dot_nt(q, k) + bias
    m_new = jnp.maximum(m, jnp.max(s, axis=-1, keepdims=True))
    alpha = jnp.exp2(m - m_new)
    p = jnp.exp2(s - m_new)
    return m_new, alpha * l + jnp.sum(p, axis=-1, keepdims=True), alpha * acc + _dot(p.astype(BF16), v)


def _flash_init(n):
    return jnp.full((n, 1), NEG, F32), jnp.zeros((n, 1), F32), jnp.zeros((n, LANES), F32)


def _dsa_attn_body(q_ref, k_ref, v_ref, km_ref, vm_ref, bm_ref, bf_ref, o_ref, *, tq, tb, group, heads):
    qi = pl.program_id(2)
    lanes = [slice(u * LANES, (u + 1) * LANES) for u in range(heads)]
    qs = [_dsa_query(q_ref[0, :, ln]) for ln in lanes]

    def update(states, k, v, bias):
        return tuple(_masked_flash_update(qs[u], k[:, lanes[u]].astype(BF16), v[:, lanes[u]].astype(BF16), bias,
                                          states[u]) for u in range(heads))

    def tiles(j0, n):
        start = pl.multiple_of(j0 * tb, tb)
        bias = jnp.concatenate([bf_ref[0, j0 + u].astype(F32) for u in range(n)], axis=1)
        return k_ref[0, pl.ds(start, n * tb), :], v_ref[0, pl.ds(start, n * tb), :], bias

    states = update((_flash_init(tq),) * heads, km_ref[...], vm_ref[...], bm_ref[0].astype(F32))
    n_tiles = (qi * tq + tq - 1) // tb + 1
    n_wide = n_tiles // group
    states = lax.fori_loop(0, n_wide, lambda g, st: update(st, *tiles(g * group, group)), states)
    states = lax.fori_loop(n_wide * group, n_tiles, lambda j, st: update(st, *tiles(j, 1)), states)
    for u in range(heads):
        m, l, acc = states[u]
        o_ref[0, :, lanes[u]] = acc * (1.0 / l)


def _dsa_attn(bq, bk, bv, bkm, bvm, bias_m, bias_f, *, tq):
    bsz, t, _ = bq.shape
    nb, tb = bias_f.shape[1], bias_f.shape[3]
    heads = 2
    group = max(1, min(2048, t) // tb)
    width = heads * LANES
    head_q = lambda b, h, i: (b, i, h)
    head_all = lambda b, h, i: (b, 0, h)
    head_meta = lambda b, h, i: (0, h)
    return pl.pallas_call(
        functools.partial(_dsa_attn_body, tq=tq, tb=tb, group=group, heads=heads),
        grid=(bsz, N_HEADS // heads, t // tq),
        in_specs=[
            pl.BlockSpec((1, tq, width), head_q),
            pl.BlockSpec((1, t, width), head_all),
            pl.BlockSpec((1, t, width), head_all),
            pl.BlockSpec((META_PAD, width), head_meta),
            pl.BlockSpec((META_PAD, width), head_meta),
            pl.BlockSpec((1, tq, LANES), lambda b, h, i: (b, i, 0)),
            pl.BlockSpec((1, nb, tq, tb), lambda b, h, i: (b, 0, i, 0)),
        ],
        out_specs=pl.BlockSpec((1, tq, width), head_q),
        out_shape=jax.ShapeDtypeStruct(bq.shape, F32),
        compiler_params=_params(),
        name="dsa_attn_prompt",
    )(bq, bk, bv, bkm, bvm, bias_m, bias_f)


def _dsa_sample_body(iq_ref, wq_ref, kidx_ref, q_ref, kc_ref, vc_ref, kn_ref, vn_ref, o_ref, *,
                     s_new, past, n_top, idx_bits):
    qs = _stack_index_queries(iq_ref[...], IDX_DIM)
    w = wq_ref[...]
    ik_past = kidx_ref[0, N_META:N_META + past, :]
    ik_small = jnp.concatenate(
        [kidx_ref[0, 0:N_META, :], w[:, 0:IDX_DIM], jnp.zeros((META_PAD - N_META - s_new, IDX_DIM), F32)], axis=0)
    lane_s = lax.broadcasted_iota(jnp.int32, (s_new, META_PAD), 1)
    sc_s = _index_scores(qs, w, ik_small, s_new)
    key_s = _sort_key(jnp.where(lane_s < N_META + s_new, sc_s, -jnp.inf))
    key_p = _sort_key(_index_scores(qs, w, ik_past, s_new))
    col_s = jnp.where(lane_s < N_META, lane_s, lane_s + past)
    col_p = N_META + lax.broadcasted_iota(jnp.int32, (s_new, past), 1)

    every = slice(None)
    count_fn = lambda f: jnp.sum(
        f(key_s, col_s, every) + _lane_sum(f(key_p, col_p, every)), axis=-1, keepdims=True)
    thr, cut = _select_threshold(count_fn, s_new, n_top, idx_bits)
    bias_s = _select_bias(key_s, col_s, thr, cut)
    bias_p = _select_bias(key_p, col_p, thr, cut)

    for h in range(N_HEADS):
        sl = slice(h * LANES, (h + 1) * LANES)
        q = _dsa_query(q_ref[:, sl])
        state = _masked_flash_update(q, _small_keys(kc_ref, kn_ref, h, s_new).astype(BF16),
                                     _small_keys(vc_ref, vn_ref, h, s_new).astype(BF16), bias_s,
                                     _flash_init(s_new))
        m, l, acc = _masked_flash_update(q, _cache_head(kc_ref, h, N_META, past).astype(BF16),
                                         _cache_head(vc_ref, h, N_META, past).astype(BF16), bias_p, state)
        o_ref[:, sl] = acc * (1.0 / l)


def _dsa_sample(iq, misc, bq, bk, bv, cache_kidx, cache_k, cache_v, *, s_new, n_top):
    bsz, n_rows, _ = cache_k.shape
    width = bq.shape[1]
    n_cache = n_rows // N_HEADS
    past = n_cache - N_META
    idx_bits = (n_cache + s_new - 1).bit_length()
    rows = lambda b: (b, 0)
    cache = lambda b: (b, 0, 0)
    return pl.pallas_call(
        functools.partial(_dsa_sample_body, s_new=s_new, past=past, n_top=n_top, idx_bits=idx_bits),
        grid=(bsz,),
        in_specs=[
            pl.BlockSpec((s_new, IQ_WIDTH), rows),
            pl.BlockSpec((s_new, LANES), rows),
            pl.BlockSpec((1, n_cache, IDX_DIM), cache),
            pl.BlockSpec((s_new, width), rows),
            pl.BlockSpec((1, n_rows, LANES), cache),
            pl.BlockSpec((1, n_rows, LANES), cache),
            pl.BlockSpec((s_new, width), rows),
            pl.BlockSpec((s_new, width), rows),
        ],
        out_specs=pl.BlockSpec((s_new, width), rows),
        out_shape=jax.ShapeDtypeStruct(bq.shape, F32),
        compiler_params=_params(),
        name="dsa_sample",
    )(iq, misc, cache_kidx, bq, cache_k, cache_v, bk, bv)


def _merge_body(x_ref, oa_ref, ob_ref, g_ref, wa_ref, wb_ref, wo_ref, o_ref):
    ya = _dot(oa_ref[...].astype(BF16), wa_ref[...])
    yb = _dot(ob_ref[...].astype(BF16), wb_ref[...])
    merged = jax.nn.sigmoid(g_ref[:, :D_MODEL]) * ya + jax.nn.sigmoid(g_ref[:, D_MODEL:]) * yb
    o_ref[...] = x_ref[...] + _dot(merged.astype(BF16), wo_ref[...])


def _merge(x, oa, ob, gates, wa, wb, wo, *, tm):
    n, d = x.shape
    row = lambda i: (i, 0)
    const = lambda i: (0, 0)
    return pl.pallas_call(
        _merge_body,
        grid=(n // tm,),
        in_specs=[
            pl.BlockSpec((tm, d), row),
            pl.BlockSpec((tm, d), row),
            pl.BlockSpec((tm, d), row),
            pl.BlockSpec((tm, 2 * d), row),
            pl.BlockSpec(wa.shape, const),
            pl.BlockSpec(wb.shape, const),
            pl.BlockSpec(wo.shape, const),
        ],
        out_specs=pl.BlockSpec((tm, d), row),
        out_shape=jax.ShapeDtypeStruct((n, d), F32),
        compiler_params=_params(),
        name="merge",
    )(x, oa, ob, gates, wa, wb, wo)


def _row_tile(n, pref):
    tm = min(pref, n)
    assert n % tm == 0, (n, tm)
    return tm


def kernel(x_prompt, x_sample, cache_a_k, cache_a_v, cache_b_k, cache_b_v, cache_b_kidx, meta, g_ffn1, w1_gate,
           w1_up, w1_down, g_mix, w_in, lam_q1, lam_k1, lam_q2, lam_k2, a_subln, w_a, w_b, w_o, g_ffn2, w2_gate,
           w2_up, w2_down, g_final):
    bsz, seq, d = x_prompt.shape
    dec_b, dec_s, _ = x_sample.shape
    n_cache = cache_a_k.shape[2]
    past = n_cache - N_META
    assert d == D_MODEL and meta.shape == (N_META, D_MODEL)
    assert cache_a_k.shape[0] == 1, "single-layer step"
    assert past % LANES == 0 and (past % CHUNK) + dec_s <= CHUNK, "all cached and new keys visible to every new query"
    n_top_p = min(TOPK_MAX, seq // 4)
    n_top_s = min(TOPK_MAX, (past + dec_s) // 4)

    cast = lambda w: w[0].astype(BF16)
    w1g, w1u, w1d = cast(w1_gate), cast(w1_up), cast(w1_down)
    w2g, w2u, w2d = cast(w2_gate), cast(w2_up), cast(w2_down)
    wa, wb, wo = cast(w_a), cast(w_b), cast(w_o)
    win = w_in[0]
    c0 = 3 * A_WIDTH
    c1 = c0 + 3 * B_WIDTH
    c2 = c1 + IQ_WIDTH
    c3 = c2 + IDX_DIM + IDX_HEADS
    w_pa = win[:, :c0].astype(BF16)
    w_pb = win[:, c0:c1].astype(BF16)
    w_pig = jnp.concatenate(
        [win[:, c1:c3], jnp.zeros((d, LANES - IDX_DIM - IDX_HEADS), F32), win[:, c3:]], axis=1).astype(BF16)
    g1, gm, g2, gf = g_ffn1[0][None], g_mix[0][None], g_ffn2[0][None], g_final[None]
    lam4 = jnp.stack([lam_q1[0], lam_k1[0], lam_q2[0], lam_k2[0]]).astype(F32)
    gsub = a_subln[0][None]

    def front(x, pos, tm_ffn, tm_proj, cache_meta=None):
        x1, h = _ffn(x, g1, w1g, w1u, w1d, gm, tm=tm_ffn, emit_h=True)
        proj, cache_rows = _project_all(h, pos, w_pa, w_pb, w_pig, tm=tm_proj, cache_meta=cache_meta)
        return (x1,) + proj, cache_rows

    def back(x1, oa, ob, gates, tm, tm_ffn):
        x2 = _merge(x1, oa, ob, gates, wa, wb, wo, tm=tm)
        (y,) = _ffn(x2, g2, w2g, w2u, w2d, gf, tm=tm_ffn, emit_h=False)
        return y

    (_, _, ak_m, av_m, _, bk_m, bv_m, _, misc_m, _), _ = front(
        meta, jnp.arange(N_META, dtype=jnp.int32), N_META, N_META)
    pad_meta = lambda a: jnp.pad(a, ((0, META_PAD - N_META), (0, 0)))

    n_p = bsz * seq
    tm_p = _row_tile(seq, 512)
    tm_ffn_p = _row_tile(n_p, 1024)
    pos_p = N_META + jnp.arange(seq, dtype=jnp.int32)
    (x1_p, aq_p, ak_p, av_p, bq_p, bk_p, bv_p, iq_p, misc_p, gates_p), cache_rows_p = front(
        x_prompt.reshape(n_p, d), pos_p, tm_ffn_p, tm_p, cache_meta=(seq, ak_m, av_m, bk_m, bv_m))
    r3 = lambda a: a.reshape(bsz, seq, a.shape[-1])
    tq = _row_tile(seq, 512)
    oa_p = _diff_prompt(r3(aq_p), r3(ak_p), r3(av_p), pad_meta(ak_m), pad_meta(av_m), lam4, gsub, tq=tq)
    bias_m, bias_f = _dsa_index(r3(iq_p), r3(misc_p), pad_meta(misc_m), tq=tq, tk=tq,
                                n_top=n_top_p)
    ob_p = _dsa_attn(r3(bq_p), r3(bk_p), r3(bv_p), pad_meta(bk_m), pad_meta(bv_m), bias_m, bias_f, tq=tq)
    y_p = back(x1_p, oa_p.reshape(n_p, d), ob_p.reshape(n_p, d), gates_p, tm_p, tm_ffn_p).reshape(bsz, seq, d)

    n_s = dec_b * dec_s
    pos_s = N_META + past + jnp.tile(jnp.arange(dec_s, dtype=jnp.int32), dec_b)
    (x1_s, aq_s, ak_s, av_s, bq_s, bk_s, bv_s, iq_s, misc_s, gates_s), _ = front(
        x_sample.reshape(n_s, d), pos_s, n_s, n_s)
    c3d = lambda c: c[0].reshape(dec_b, n_cache * N_HEADS, LANES)
    oa_s = _diff_sample(aq_s, ak_s, av_s, c3d(cache_a_k), c3d(cache_a_v), lam4, gsub, s_new=dec_s)
    ob_s = _dsa_sample(iq_s, misc_s, bq_s, bk_s, bv_s, cache_b_kidx[0], c3d(cache_b_k), c3d(cache_b_v),
                       s_new=dec_s, n_top=n_top_s)
    y_s = back(x1_s, oa_s, ob_s, gates_s, n_s, n_s).reshape(dec_b, dec_s, d)

    ha, hb, hi = (A_HEADS, 2 * A_DIM), (B_HEADS, B_DIM), (IDX_DIM,)
    rows_p = lambda a, heads: a.reshape((1, bsz, N_META + seq) + heads)
    rows_s = lambda a, heads: a.reshape((1, dec_b, dec_s) + heads)
    kidx_p = jnp.concatenate(
        [jnp.broadcast_to(misc_m[None, :, :IDX_DIM], (bsz, N_META, IDX_DIM)), r3(misc_p)[:, :, :IDX_DIM]], axis=1)
    ak_r, av_r, bk_r, bv_r = cache_rows_p
    return (
        y_p, y_s,
        rows_p(ak_r, ha), rows_p(av_r, ha), rows_p(bk_r, hb), rows_p(bv_r, hb), kidx_p[None],
        rows_s(ak_s, ha), rows_s(av_s, ha), rows_s(bk_s, hb), rows_s(bv_s, hb), rows_s(misc_s[:, :IDX_DIM], hi),
    )
```

```python
import functools

import jax
import jax.numpy as jnp
from jax import lax
from jax.experimental import pallas as pl
from jax.experimental.pallas import tpu as pltpu

D_MODEL = 1024
CHUNK = 64
N_META = 16
ROPE_THETA = 10000.0
EPS = 1e-6
A_HEADS = 8
A_DIM = 64
B_HEADS = 8
B_DIM = 128
IDX_HEADS = 8
IDX_DIM = 64
TOPK_MAX = 256
LAM_INIT = 0.2

LANES = 128
N_HEADS = 8
A_WIDTH = A_HEADS * 2 * A_DIM
B_WIDTH = B_HEADS * B_DIM
IQ_WIDTH = IDX_HEADS * IDX_DIM
META_PAD = LANES
SEARCH_ROWS = 128

VMEM_LIMIT = 56 * 1024 * 1024
NEG = -1e30
INT_MIN = -(2 ** 31)
KEY_NEG_INF = -2139095041
HI_NEG_INF = KEY_NEG_INF >> 16
I16_MIN = -(2 ** 15)
LOG2E = 1.4426950408889634

F32 = jnp.float32
BF16 = jnp.bfloat16
I16 = jnp.int16

_NT = (((1,), (1,)), ((), ()))


def _params():
    return pltpu.CompilerParams(vmem_limit_bytes=VMEM_LIMIT)


def _dot(a, b):
    return jnp.dot(a, b, preferred_element_type=F32)


def _dot_nt(a, b):
    return lax.dot_general(a, b, _NT, preferred_element_type=F32)


def _rms(x, g):
    ms = jnp.mean(x * x, axis=-1, keepdims=True)
    return x * lax.rsqrt(ms + EPS) * g


def _ffn_body(x_ref, g_ref, wg_ref, wu_ref, wd_ref, gn_ref, *rest, n_ff, emit_h):
    if emit_h:
        o_ref, hn_ref, h_scr, acc_scr = rest
    else:
        o_ref, h_scr, acc_scr = rest
    j = pl.program_id(1)

    @pl.when(j == 0)
    def _():
        h_scr[...] = _rms(x_ref[...], g_ref[...]).astype(BF16)
        acc_scr[...] = jnp.zeros_like(acc_scr)

    h = h_scr[...]
    gate = _dot(h, wg_ref[...])
    up = _dot(h, wu_ref[...])
    act = gate * jax.nn.sigmoid(gate) * up
    acc_scr[...] += _dot(act.astype(BF16), wd_ref[...])

    @pl.when(j == n_ff - 1)
    def _():
        y = x_ref[...] + 0.5 * acc_scr[...]
        if emit_h:
            o_ref[...] = y
            hn_ref[...] = _rms(y, gn_ref[...]).astype(BF16)
        else:
            o_ref[...] = _rms(y, gn_ref[...])


def _ffn(x, g, wg, wu, wd, gn, *, tm, emit_h):
    n, d = x.shape
    dff = wg.shape[1]
    tf = 256 if dff % 256 == 0 else LANES
    n_ff = dff // tf
    row = lambda i, j: (i, 0)
    const = lambda i, j: (0, 0)
    out_shape = [jax.ShapeDtypeStruct((n, d), F32)]
    out_specs = [pl.BlockSpec((tm, d), row)]
    if emit_h:
        out_shape.append(jax.ShapeDtypeStruct((n, d), BF16))
        out_specs.append(pl.BlockSpec((tm, d), row))
    return pl.pallas_call(
        functools.partial(_ffn_body, n_ff=n_ff, emit_h=emit_h),
        grid=(n // tm, n_ff),
        in_specs=[
            pl.BlockSpec((tm, d), row),
            pl.BlockSpec((1, d), const),
            pl.BlockSpec((d, tf), lambda i, j: (0, j)),
            pl.BlockSpec((d, tf), lambda i, j: (0, j)),
            pl.BlockSpec((tf, d), lambda i, j: (j, 0)),
            pl.BlockSpec((1, d), const),
        ],
        out_specs=out_specs,
        out_shape=out_shape,
        scratch_shapes=[pltpu.VMEM((tm, d), BF16), pltpu.VMEM((tm, d), F32)],
        compiler_params=_params(),
        name="ffn_h" if emit_h else "ffn_final",
    )(x, g, wg, wu, wd, gn)


def _rope_block(z, tab_ref, kind, t0):
    cos = tab_ref[:, t0 * LANES:(t0 + 1) * LANES]
    s_a = tab_ref[:, (t0 + 1) * LANES:(t0 + 2) * LANES]
    if kind == "d128":
        return z * cos + pltpu.roll(z, 64, 1) * s_a
    s_b = tab_ref[:, (t0 + 2) * LANES:(t0 + 3) * LANES]
    return z * cos + pltpu.roll(z, 96, 1) * s_a + pltpu.roll(z, 32, 1) * s_b


def _proj_body(h_ref, w_ref, tab_ref, *refs, segs, cache_segs, tm, tiles_per_batch, n_steps):
    n_c = len(cache_segs)
    meta_refs = refs[:n_c]
    out_refs = refs[n_c:n_c + len(segs)]
    cache_refs = refs[n_c + len(segs):2 * n_c + len(segs)]
    if n_c:
        stage, meta_stage, sem, meta_sem = refs[2 * n_c + len(segs):]
    i = pl.program_id(0)
    slot = i % 2
    batch = i // tiles_per_batch
    tile = i % tiles_per_batch

    def frames_copy(c, s):
        first = (N_META + tile * tm) * N_HEADS
        return pltpu.make_async_copy(
            stage.at[c, s], cache_refs[c].at[batch, pl.ds(first, tm * N_HEADS), :], sem.at[c, s])

    if n_c:
        @pl.when(i >= 2)
        def _():
            for c in range(n_c):
                frames_copy(c, slot).wait()

    h = h_ref[...]
    for out_idx, col0, width, rope in segs:
        cw = 256 if width % 256 == 0 else LANES
        for c in range(0, width, cw):
            z = _dot(h, w_ref[:, col0 + c:col0 + c + cw])
            for b in range(cw // LANES):
                zb = z[:, b * LANES:(b + 1) * LANES]
                if rope is not None:
                    zb = _rope_block(zb, tab_ref, rope[0], rope[1])
                out_refs[out_idx][:, c + b * LANES:c + (b + 1) * LANES] = zb.astype(out_refs[out_idx].dtype)
                if out_idx in cache_segs:
                    head = c // LANES + b
                    stage[cache_segs.index(out_idx), slot, pl.ds(head, tm, stride=N_HEADS), :] = zb

    for c in range(n_c):
        frames_copy(c, slot).start()

    if n_c:
        @pl.when(tile == 0)
        def _():
            for c in range(n_c):
                for head in range(N_HEADS):
                    meta_stage[c, pl.ds(head, N_META, stride=N_HEADS), :] = (
                        meta_refs[c][:, head * LANES:(head + 1) * LANES])
                copy = pltpu.make_async_copy(
                    meta_stage.at[c], cache_refs[c].at[batch, pl.ds(0, N_META * N_HEADS), :], meta_sem.at[c])
                copy.start()
                copy.wait()

        @pl.when(i == n_steps - 1)
        def _():
            for c in range(n_c):
                frames_copy(c, slot).wait()
                if n_steps > 1:
                    frames_copy(c, 1 - slot).wait()


def _proj(h, w, tab, segs, *, tm, name, cache=None):
    n, d = h.shape
    widths = [s[2] for s in segs]
    row = lambda i: (i, 0)
    const = lambda i: (0, 0)
    n_steps = n // tm
    seq, meta_rows = cache if cache is not None else (n, {})
    cache_segs = tuple(sorted(meta_rows))
    n_c = len(cache_segs)
    assert seq % tm == 0 and all(widths[s] == N_HEADS * LANES for s in cache_segs)
    assert tab.shape[0] % tm == 0 and n % tab.shape[0] == 0
    tab_tiles = tab.shape[0] // tm
    out_specs = [pl.BlockSpec((tm, wd), row) for wd in widths]
    out_shape = [jax.ShapeDtypeStruct((n, wd), BF16 if s in cache_segs else F32) for s, wd in enumerate(widths)]
    out_specs += [pl.BlockSpec(memory_space=pl.ANY)] * n_c
    out_shape += [jax.ShapeDtypeStruct((n // seq, (N_META + seq) * N_HEADS, LANES), F32)] * n_c
    scratch = []
    if n_c:
        scratch = [pltpu.VMEM((n_c, 2, tm * N_HEADS, LANES), F32), pltpu.VMEM((n_c, N_META * N_HEADS, LANES), F32),
                   pltpu.SemaphoreType.DMA((n_c, 2)), pltpu.SemaphoreType.DMA((n_c,))]
    return pl.pallas_call(
        functools.partial(_proj_body, segs=tuple(segs), cache_segs=cache_segs, tm=tm,
                          tiles_per_batch=seq // tm, n_steps=n_steps),
        grid=(n_steps,),
        in_specs=[
            pl.BlockSpec((tm, d), row),
            pl.BlockSpec(w.shape, const),
            pl.BlockSpec((tm, tab.shape[1]), lambda i: (i % tab_tiles, 0)),
        ] + [pl.BlockSpec((N_META, widths[s]), const) for s in cache_segs],
        out_specs=out_specs,
        out_shape=out_shape,
        scratch_shapes=scratch,
        compiler_params=_params(),
        name=name,
    )(h, w, tab, *[meta_rows[s] for s in cache_segs])


def _rope_tables(pos):
    pos = pos.astype(F32)
    inv32 = ROPE_THETA ** (-jnp.arange(32, dtype=F32) / 32)
    inv64 = ROPE_THETA ** (-jnp.arange(64, dtype=F32) / 64)
    a32 = pos[:, None] * inv32[None, :]
    a64 = pos[:, None] * inv64[None, :]
    c32, s32 = jnp.cos(a32), jnp.sin(a32)
    c64, s64 = jnp.cos(a64), jnp.sin(a64)
    z32 = jnp.zeros_like(c32)
    z64 = jnp.zeros_like(c64)
    return jnp.concatenate(
        [c32, c32, c32, c32, -s32, z32, -s32, z32, z32, s32, z32, s32,
         c64, c64, -s64, s64,
         c32, c32, jnp.ones_like(c64), -s32, z32, z64, z32, s32, z64],
        axis=1)


TAB_D64, TAB_D128, TAB_MISC = 0, 3, 5


def _project_all(h, pos, w_a, w_b, w_ig, *, tm, cache_meta=None):
    tab = _rope_tables(pos)
    cache_a = cache_b = None
    if cache_meta is not None:
        seq, ak_m, av_m, bk_m, bv_m = cache_meta
        cache_a, cache_b = (seq, {1: ak_m, 2: av_m}), (seq, {1: bk_m, 2: bv_m})
    aq, ak, av, *rows_a = _proj(h, w_a, tab,
                                [(0, 0, A_WIDTH, ("d64", TAB_D64)), (1, A_WIDTH, A_WIDTH, ("d64", TAB_D64)),
                                 (2, 2 * A_WIDTH, A_WIDTH, None)], tm=tm, name="proj_a", cache=cache_a)
    bq, bk, bv, *rows_b = _proj(h, w_b, tab,
                                [(0, 0, B_WIDTH, ("d128", TAB_D128)), (1, B_WIDTH, B_WIDTH, ("d128", TAB_D128)),
                                 (2, 2 * B_WIDTH, B_WIDTH, None)], tm=tm, name="proj_b", cache=cache_b)
    iq, misc, gates = _proj(h, w_ig, tab,
                            [(0, 0, IQ_WIDTH, ("d64", TAB_D64)), (1, IQ_WIDTH, LANES, ("d64", TAB_MISC)),
                             (2, IQ_WIDTH + LANES, 2 * D_MODEL, None)], tm=tm, name="proj_ig")
    return (aq, ak, av, bq, bk, bv, iq, misc, gates), tuple(rows_a + rows_b)


def _lam(lam_ref):
    lp = lam_ref[...]
    s1 = jnp.sum(lp[0:1] * lp[1:2], axis=-1, keepdims=True)
    s2 = jnp.sum(lp[2:3] * lp[3:4], axis=-1, keepdims=True)
    return jnp.exp(s1) - jnp.exp(s2) + LAM_INIT


def _stack_diff_queries(q):
    lane = lax.broadcasted_iota(jnp.int32, q.shape, 1)
    q = q * (A_DIM ** -0.5 * LOG2E)
    q1 = jnp.where(lane < A_DIM, q, 0.0)
    q2 = jnp.where(lane >= A_DIM, q, 0.0)
    return jnp.concatenate([q1, q2], axis=0).astype(BF16)


def _diff_finish(acc, l, n, lam, g):
    o1 = acc[:n] * (1.0 / l[:n])
    o2 = acc[n:] * (1.0 / l[n:])
    o = o1 - lam * o2
    return _rms(o, g) * (1.0 - LAM_INIT)


def _diff_prompt_body(lam_ref, q_ref, k_ref, v_ref, km_ref, vm_ref, g_ref, o_ref, *, tq, tk, heads):
    qi = pl.program_id(2)
    lanes = [slice(u * LANES, (u + 1) * LANES) for u in range(heads)]
    qs = [_stack_diff_queries(q_ref[0, :, ln]) for ln in lanes]

    def update_head(q, state, k, v, visible):
        m, l, acc = state
        s = _dot_nt(q, k)
        if visible is not None:
            s = jnp.where(visible, s, NEG)
        m_new = jnp.maximum(m, jnp.max(s, axis=-1, keepdims=True))
        alpha = jnp.exp2(m - m_new)
        p = jnp.exp2(s - m_new)
        return m_new, alpha * l + jnp.sum(p, axis=-1, keepdims=True), alpha * acc + _dot(p.astype(BF16), v)

    def update(states, k, v, visible):
        return tuple(update_head(qs[u], states[u], k[:, lanes[u]].astype(BF16), v[:, lanes[u]].astype(BF16), visible)
                     for u in range(heads))

    def load(start, n):
        return k_ref[0, pl.ds(start, n), :], v_ref[0, pl.ds(start, n), :]

    col = lax.broadcasted_iota(jnp.int32, (2 * tq, META_PAD), 1)
    init = (jnp.full((2 * tq, 1), NEG, F32), jnp.zeros((2 * tq, 1), F32), jnp.zeros((2 * tq, LANES), F32))
    states = update((init,) * heads, km_ref[...], vm_ref[...], col < N_META)

    n_wide = (qi * tq) // tk
    states = lax.fori_loop(
        0, n_wide, lambda j, st: update(st, *load(pl.multiple_of(j * tk, tk), tk), None), states)
    states = lax.fori_loop(
        n_wide * (tk // tq), qi, lambda j, st: update(st, *load(pl.multiple_of(j * tq, tq), tq), None), states)
    r = lax.broadcasted_iota(jnp.int32, (2 * tq, tq), 0)
    c = lax.broadcasted_iota(jnp.int32, (2 * tq, tq), 1)
    r = jnp.where(r >= tq, r - tq, r)
    states = update(states, *load(pl.multiple_of(qi * tq, tq), tq), c // CHUNK <= r // CHUNK)
    lam = _lam(lam_ref)
    for u in range(heads):
        m, l, acc = states[u]
        o_ref[0, :, lanes[u]] = _diff_finish(acc, l, tq, lam, g_ref[...])


def _diff_prompt(aq, ak, av, akm, avm, lam4, g, *, tq):
    bsz, t, _ = aq.shape
    heads = 2
    tk = min(4 * tq, t)
    width = heads * LANES
    head_q = lambda b, h, i: (b, i, h)
    head_all = lambda b, h, i: (b, 0, h)
    head_meta = lambda b, h, i: (0, h)
    const = lambda b, h, i: (0, 0)
    return pl.pallas_call(
        functools.partial(_diff_prompt_body, tq=tq, tk=tk, heads=heads),
        grid=(bsz, N_HEADS // heads, t // tq),
        in_specs=[
            pl.BlockSpec(lam4.shape, const),
            pl.BlockSpec((1, tq, width), head_q),
            pl.BlockSpec((1, t, width), head_all),
            pl.BlockSpec((1, t, width), head_all),
            pl.BlockSpec((META_PAD, width), head_meta),
            pl.BlockSpec((META_PAD, width), head_meta),
            pl.BlockSpec((1, LANES), const),
        ],
        out_specs=pl.BlockSpec((1, tq, width), head_q),
        out_shape=jax.ShapeDtypeStruct(aq.shape, F32),
        compiler_params=_params(),
        name="diff_attn_prompt",
    )(lam4, aq, ak, av, akm, avm, g)


def _cache_head(cache_ref, h, first, n):
    return cache_ref[0, pl.ds(first * N_HEADS + h, n, stride=N_HEADS), :]


def _small_keys(cache_ref, new_ref, h, s_new):
    meta = _cache_head(cache_ref, h, 0, N_META)
    new = new_ref[:, h * LANES:(h + 1) * LANES]
    pad = jnp.zeros((META_PAD - N_META - s_new, LANES), F32)
    return jnp.concatenate([meta, new, pad], axis=0)


def _diff_sample_body(lam_ref, q_ref, kc_ref, vc_ref, kn_ref, vn_ref, g_ref, o_ref, *, s_new, past):
    lam = _lam(lam_ref)
    for h in range(N_HEADS):
        sl = slice(h * LANES, (h + 1) * LANES)
        qs = _stack_diff_queries(q_ref[:, sl])
        kp = _cache_head(kc_ref, h, N_META, past).astype(BF16)
        vp = _cache_head(vc_ref, h, N_META, past).astype(BF16)
        ks = _small_keys(kc_ref, kn_ref, h, s_new).astype(BF16)
        vs = _small_keys(vc_ref, vn_ref, h, s_new).astype(BF16)
        s_p = _dot_nt(qs, kp)
        s_s = _dot_nt(qs, ks)
        col = lax.broadcasted_iota(jnp.int32, s_s.shape, 1)
        s_s = jnp.where(col < N_META + s_new, s_s, NEG)
        m = jnp.maximum(jnp.max(s_p, axis=-1, keepdims=True), jnp.max(s_s, axis=-1, keepdims=True))
        p_p = jnp.exp2(s_p - m)
        p_s = jnp.exp2(s_s - m)
        l = jnp.sum(p_p, axis=-1, keepdims=True) + jnp.sum(p_s, axis=-1, keepdims=True)
        acc = _dot(p_p.astype(BF16), vp) + _dot(p_s.astype(BF16), vs)
        o_ref[:, sl] = _diff_finish(acc, l, s_new, lam, g_ref[...])


def _diff_sample(aq, ak, av, cache_k, cache_v, lam4, g, *, s_new):
    bsz, n_rows, _ = cache_k.shape
    width = aq.shape[1]
    past = n_rows // N_HEADS - N_META
    rows = lambda b: (b, 0)
    cache = lambda b: (b, 0, 0)
    const = lambda b: (0, 0)
    return pl.pallas_call(
        functools.partial(_diff_sample_body, s_new=s_new, past=past),
        grid=(bsz,),
        in_specs=[
            pl.BlockSpec(lam4.shape, const),
            pl.BlockSpec((s_new, width), rows),
            pl.BlockSpec((1, n_rows, LANES), cache),
            pl.BlockSpec((1, n_rows, LANES), cache),
            pl.BlockSpec((s_new, width), rows),
            pl.BlockSpec((s_new, width), rows),
            pl.BlockSpec((1, LANES), const),
        ],
        out_specs=pl.BlockSpec((s_new, width), rows),
        out_shape=jax.ShapeDtypeStruct(aq.shape, F32),
        compiler_params=_params(),
        name="diff_attn_sample",
    )(lam4, aq, cache_k, cache_v, ak, av, g)


def _stack_index_queries(iq, width):
    n = iq.shape[0]
    lane = lax.broadcasted_iota(jnp.int32, (n, LANES), 1)
    parts = []
    for h in range(IDX_HEADS):
        blk = iq[:, (h // 2) * LANES:(h // 2 + 1) * LANES]
        if h % 2:
            blk = pltpu.roll(blk, IDX_DIM, 1)
        blk = jnp.where(lane < IDX_DIM, blk, 0.0) * (IDX_DIM ** -0.5)
        parts.append(blk[:, :width])
    return jnp.concatenate(parts, axis=0).astype(BF16)


def _index_scores(qs, w, keys, n):
    rel = jnp.maximum(_dot_nt(qs, keys.astype(BF16)), 0.0)
    sc = rel[0:n] * w[:, IDX_DIM:IDX_DIM + 1]
    for h in range(1, IDX_HEADS):
        sc = sc + rel[h * n:(h + 1) * n] * w[:, IDX_DIM + h:IDX_DIM + h + 1]
    return sc * (IDX_HEADS ** -0.5)


def _sort_key(x):
    bits = lax.bitcast_convert_type(x, jnp.int32)
    key = bits ^ ((bits >> 31) & 0x7FFFFFFF)
    return jnp.where(key == -1, 0, key)


def _ind(pred):
    return jnp.where(pred, 1.0, 0.0)


def _lane_sum(x):
    acc = x[:, 0:LANES]
    for b in range(1, x.shape[1] // LANES):
        acc = acc + x[:, b * LANES:(b + 1) * LANES]
    return acc


def _select_threshold(count_fn, n_rows, n_top, idx_bits):
    def bit_step(i, thr):
        cand = thr ^ jnp.left_shift(jnp.int32(1), 31 - i)
        cnt = count_fn(lambda key, col, rows: _ind(key >= cand[rows]))
        return jnp.where(cnt >= n_top, cand, thr)

    thr = lax.fori_loop(0, 32, bit_step, jnp.full((n_rows, 1), INT_MIN, jnp.int32))
    need = n_top - count_fn(lambda key, col, rows: _ind(key > thr[rows]))
    excess = count_fn(lambda key, col, rows: _ind(key == thr[rows])) - need

    def tie_search(_):
        def idx_step(i, cut):
            cand = cut | jnp.left_shift(jnp.int32(1), idx_bits - 1 - i)
            cnt = count_fn(lambda key, col, rows: jnp.where(key == thr[rows], _ind(col < cand[rows]), 0.0))
            return jnp.where(cnt < need, cand, cut)
        return lax.fori_loop(0, idx_bits, idx_step, jnp.zeros((n_rows, 1), jnp.int32))

    no_ties = lambda _: jnp.full((n_rows, 1), 2 ** 30, jnp.int32)
    cut = lax.cond(jnp.max(excess) > 0.0, tie_search, no_ties, None)
    return thr, cut


def _select_bias(key, col, thr, cut):
    tie = jnp.where(key == thr, jnp.where(col <= cut, 0.0, NEG), NEG)
    bias = jnp.where(key > thr, 0.0, tie)
    return jnp.where(key > KEY_NEG_INF, bias, NEG)


def _split_key(key):
    hi = (key >> 16).astype(I16)
    lo = ((key & 0xFFFF) - 32768).astype(I16)
    return hi, lo


def _ind16(pred):
    return jnp.where(pred, jnp.ones(pred.shape, I16), jnp.zeros(pred.shape, I16))


def _dsa_index_body(iq_ref, wq_ref, ik_ref, ikm_ref, om_ref, of_ref, hi_m, lo_m, hi_f, lo_f, *, tq, tk,
                    n_tiles_all, n_top):
    qi = pl.program_id(1)
    qs = _stack_index_queries(iq_ref[0], LANES)
    w = wq_ref[0]

    sc = _index_scores(qs, w, ikm_ref[...], tq)
    lane = lax.broadcasted_iota(jnp.int32, (tq, LANES), 1)
    hi_m[...], lo_m[...] = _split_key(_sort_key(jnp.where(lane < N_META, sc, -jnp.inf)))

    row0 = qi * tq
    n_tiles = (row0 + tq - 1) // tk + 1
    rowc = (row0 + lax.broadcasted_iota(jnp.int32, (tq, tk), 0)) // CHUNK
    colt = lax.broadcasted_iota(jnp.int32, (tq, tk), 1)

    def score_tile(j, masked):
        start = pl.multiple_of(j * tk, tk)
        sc = _index_scores(qs, w, ik_ref[0, pl.ds(start, tk), :], tq)
        if masked:
            sc = jnp.where((start + colt) // CHUNK <= rowc, sc, -jnp.inf)
        hi_f[j], lo_f[j] = _split_key(_sort_key(sc))
        return 0

    n_open = row0 // tk
    lax.fori_loop(0, n_open, lambda j, _: score_tile(j, False), 0)
    lax.fori_loop(n_open, n_tiles, lambda j, _: score_tile(j, True), 0)

    def count(f):
        chunks = [slice(r0, r0 + SEARCH_ROWS) for r0 in range(0, tq, SEARCH_ROWS)]

        def add_tile(j, accs):
            return tuple(a + _lane_sum(f(hi_f[j, rows, :], lo_f[j, rows, :], rows)) for a, rows in zip(accs, chunks))

        accs = lax.fori_loop(0, n_tiles, add_tile, tuple(f(hi_m[rows, :], lo_m[rows, :], rows) for rows in chunks))
        return jnp.sum(jnp.concatenate(accs, axis=0).astype(F32), axis=-1, keepdims=True)

    def half_search(f_ge, need):
        def step(i, t):
            cand = t + jnp.left_shift(jnp.int32(1), 15 - i)
            c16 = cand.astype(I16)
            return jnp.where(count(f_ge(c16)) >= need, cand, t)
        return lax.fori_loop(0, 16, step, jnp.full((tq, 1), I16_MIN, jnp.int32)).astype(I16)

    th = half_search(lambda c: lambda hi, lo, rows: _ind16(hi >= c[rows]), n_top)
    need_lo = n_top - count(lambda hi, lo, rows: _ind16(hi > th[rows]))

    lo_m[...] = jnp.where(hi_m[...] == th, lo_m[...], I16_MIN)

    def tie_class_tile(j, _):
        lo_f[j] = jnp.where(hi_f[j] == th, lo_f[j], I16_MIN)
        return 0

    lax.fori_loop(0, n_tiles, tie_class_tile, 0)
    tl = half_search(lambda c: lambda hi, lo, rows: _ind16(lo >= c[rows]), need_lo)
    need = need_lo - count(lambda hi, lo, rows: _ind16(lo > tl[rows]))

    def bias(hi, lo, tri, before):
        one = jnp.ones(hi.shape, BF16)
        zero = jnp.zeros(hi.shape, BF16)
        above = jnp.where(hi > th, one, jnp.where(hi == th, jnp.where(lo > tl, one, zero), zero))
        tie = jnp.where(hi == th, jnp.where(lo == tl, one, zero), zero)
        tie = jnp.where(hi > HI_NEG_INF, tie, zero)
        rank = _dot(tie, tri)
        take = jnp.where(rank + before <= need, 1.0, 0.0).astype(BF16)
        chosen = above + tie * take
        return jnp.where(chosen > 0.5, zero, jnp.full(hi.shape, NEG, BF16)), before + rank[:, -1:]

    def upper_ones(n):
        r = lax.broadcasted_iota(jnp.int32, (n, n), 0)
        c = lax.broadcasted_iota(jnp.int32, (n, n), 1)
        return jnp.where(r <= c, 1.0, 0.0).astype(BF16)

    om_ref[0], ties = bias(hi_m[...], lo_m[...], upper_ones(META_PAD), jnp.zeros((tq, 1), F32))
    tri = upper_ones(tk)

    def write_tile(j, before):
        of_ref[0, j], after = bias(hi_f[j], lo_f[j], tri, before)
        return after

    lax.fori_loop(0, n_tiles, write_tile, ties)

    def fill_tile(j, _):
        of_ref[0, j] = jnp.full((tq, tk), NEG, BF16)
        return 0

    lax.fori_loop(n_tiles, n_tiles_all, fill_tile, 0)


def _dsa_index(iq, misc, misc_meta, *, tq, tk, n_top):
    bsz, t, _ = iq.shape
    nk = t // tk
    return pl.pallas_call(
        functools.partial(_dsa_index_body, tq=tq, tk=tk, n_tiles_all=nk, n_top=n_top),
        grid=(bsz, t // tq),
        in_specs=[
            pl.BlockSpec((1, tq, IQ_WIDTH), lambda b, i: (b, i, 0)),
            pl.BlockSpec((1, tq, LANES), lambda b, i: (b, i, 0)),
            pl.BlockSpec((1, t, LANES), lambda b, i: (b, 0, 0)),
            pl.BlockSpec((META_PAD, LANES), lambda b, i: (0, 0)),
        ],
        out_specs=[
            pl.BlockSpec((1, tq, LANES), lambda b, i: (b, i, 0)),
            pl.BlockSpec((1, nk, tq, tk), lambda b, i: (b, 0, i, 0)),
        ],
        out_shape=[
            jax.ShapeDtypeStruct((bsz, t, LANES), BF16),
            jax.ShapeDtypeStruct((bsz, nk, t, tk), BF16),
        ],
        scratch_shapes=[pltpu.VMEM((tq, LANES), I16), pltpu.VMEM((tq, LANES), I16),
                        pltpu.VMEM((nk, tq, tk), I16), pltpu.VMEM((nk, tq, tk), I16)],
        compiler_params=_params(),
        name="dsa_index",
    )(iq, misc, misc, misc_meta)


def _dsa_query(q):
    return (q * (B_DIM ** -0.5 * LOG2E)).astype(BF16)


def _masked_flash_update(q, k, v, bias, state):
    m, l, acc = state
    s = _dot_nt(q, k) + bias
    m_new = jnp.maximum(m, jnp.max(s, axis=-1, keepdims=True))
    alpha = jnp.exp2(m - m_new)
    p = jnp.exp2(s - m_new)
    return m_new, alpha * l + jnp.sum(p, axis=-1, keepdims=True), alpha * acc + _dot(p.astype(BF16), v)


def _flash_init(n):
    return jnp.full((n, 1), NEG, F32), jnp.zeros((n, 1), F32), jnp.zeros((n, LANES), F32)


def _dsa_attn_body(q_ref, k_ref, v_ref, km_ref, vm_ref, bm_ref, bf_ref, o_ref, *, tq, tb, group, heads):
    qi = pl.program_id(2)
    lanes = [slice(u * LANES, (u + 1) * LANES) for u in range(heads)]
    qs = [_dsa_query(q_ref[0, :, ln]) for ln in lanes]

    def update(states, k, v, bias):
        return tuple(_masked_flash_update(qs[u], k[:, lanes[u]].astype(BF16), v[:, lanes[u]].astype(BF16), bias,
                                          states[u]) for u in range(heads))

    def tiles(j0, n):
        start = pl.multiple_of(j0 * tb, tb)
        bias = jnp.concatenate([bf_ref[0, j0 + u].astype(F32) for u in range(n)], axis=1)
        return k_ref[0, pl.ds(start, n * tb), :], v_ref[0, pl.ds(start, n * tb), :], bias

    states = update((_flash_init(tq),) * heads, km_ref[...], vm_ref[...], bm_ref[0].astype(F32))
    n_tiles = (qi * tq + tq - 1) // tb + 1
    n_wide = n_tiles // group
    states = lax.fori_loop(0, n_wide, lambda g, st: update(st, *tiles(g * group, group)), states)
    states = lax.fori_loop(n_wide * group, n_tiles, lambda j, st: update(st, *tiles(j, 1)), states)
    for u in range(heads):
        m, l, acc = states[u]
        o_ref[0, :, lanes[u]] = acc * (1.0 / l)


def _dsa_attn(bq, bk, bv, bkm, bvm, bias_m, bias_f, *, tq):
    bsz, t, _ = bq.shape
    nb, tb = bias_f.shape[1], bias_f.shape[3]
    heads = 2
    group = max(1, min(2048, t) // tb)
    width = heads * LANES
    head_q = lambda b, h, i: (b, i, h)
    head_all = lambda b, h, i: (b, 0, h)
    head_meta = lambda b, h, i: (0, h)
    return pl.pallas_call(
        functools.partial(_dsa_attn_body, tq=tq, tb=tb, group=group, heads=heads),
        grid=(bsz, N_HEADS // heads, t // tq),
        in_specs=[
            pl.BlockSpec((1, tq, width), head_q),
            pl.BlockSpec((1, t, width), head_all),
            pl.BlockSpec((1, t, width), head_all),
            pl.BlockSpec((META_PAD, width), head_meta),
            pl.BlockSpec((META_PAD, width), head_meta),
            pl.BlockSpec((1, tq, LANES), lambda b, h, i: (b, i, 0)),
            pl.BlockSpec((1, nb, tq, tb), lambda b, h, i: (b, 0, i, 0)),
        ],
        out_specs=pl.BlockSpec((1, tq, width), head_q),
        out_shape=jax.ShapeDtypeStruct(bq.shape, F32),
        compiler_params=_params(),
        name="dsa_attn_prompt",
    )(bq, bk, bv, bkm, bvm, bias_m, bias_f)


def _dsa_sample_body(iq_ref, wq_ref, kidx_ref, q_ref, kc_ref, vc_ref, kn_ref, vn_ref, o_ref, *,
                     s_new, past, n_top, idx_bits):
    qs = _stack_index_queries(iq_ref[...], IDX_DIM)
    w = wq_ref[...]
    ik_past = kidx_ref[0, N_META:N_META + past, :]
    ik_small = jnp.concatenate(
        [kidx_ref[0, 0:N_META, :], w[:, 0:IDX_DIM], jnp.zeros((META_PAD - N_META - s_new, IDX_DIM), F32)], axis=0)
    lane_s = lax.broadcasted_iota(jnp.int32, (s_new, META_PAD), 1)
    sc_s = _index_scores(qs, w, ik_small, s_new)
    key_s = _sort_key(jnp.where(lane_s < N_META + s_new, sc_s, -jnp.inf))
    key_p = _sort_key(_index_scores(qs, w, ik_past, s_new))
    col_s = jnp.where(lane_s < N_META, lane_s, lane_s + past)
    col_p = N_META + lax.broadcasted_iota(jnp.int32, (s_new, past), 1)

    every = slice(None)
    count_fn = lambda f: jnp.sum(
        f(key_s, col_s, every) + _lane_sum(f(key_p, col_p, every)), axis=-1, keepdims=True)
    thr, cut = _select_threshold(count_fn, s_new, n_top, idx_bits)
    bias_s = _select_bias(key_s, col_s, thr, cut)
    bias_p = _select_bias(key_p, col_p, thr, cut)

    for h in range(N_HEADS):
        sl = slice(h * LANES, (h + 1) * LANES)
        q = _dsa_query(q_ref[:, sl])
        state = _masked_flash_update(q, _small_keys(kc_ref, kn_ref, h, s_new).astype(BF16),
                                     _small_keys(vc_ref, vn_ref, h, s_new).astype(BF16), bias_s,
                                     _flash_init(s_new))
        m, l, acc = _masked_flash_update(q, _cache_head(kc_ref, h, N_META, past).astype(BF16),
                                         _cache_head(vc_ref, h, N_META, past).astype(BF16), bias_p, state)
        o_ref[:, sl] = acc * (1.0 / l)


def _dsa_sample(iq, misc, bq, bk, bv, cache_kidx, cache_k, cache_v, *, s_new, n_top):
    bsz, n_rows, _ = cache_k.shape
    width = bq.shape[1]
    n_cache = n_rows // N_HEADS
    past = n_cache - N_META
    idx_bits = (n_cache + s_new - 1).bit_length()
    rows = lambda b: (b, 0)
    cache = lambda b: (b, 0, 0)
    return pl.pallas_call(
        functools.partial(_dsa_sample_body, s_new=s_new, past=past, n_top=n_top, idx_bits=idx_bits),
        grid=(bsz,),
        in_specs=[
            pl.BlockSpec((s_new, IQ_WIDTH), rows),
            pl.BlockSpec((s_new, LANES), rows),
            pl.BlockSpec((1, n_cache, IDX_DIM), cache),
            pl.BlockSpec((s_new, width), rows),
            pl.BlockSpec((1, n_rows, LANES), cache),
            pl.BlockSpec((1, n_rows, LANES), cache),
            pl.BlockSpec((s_new, width), rows),
            pl.BlockSpec((s_new, width), rows),
        ],
        out_specs=pl.BlockSpec((s_new, width), rows),
        out_shape=jax.ShapeDtypeStruct(bq.shape, F32),
        compiler_params=_params(),
        name="dsa_sample",
    )(iq, misc, cache_kidx, bq, cache_k, cache_v, bk, bv)


def _merge_body(x_ref, oa_ref, ob_ref, g_ref, wa_ref, wb_ref, wo_ref, o_ref):
    ya = _dot(oa_ref[...].astype(BF16), wa_ref[...])
    yb = _dot(ob_ref[...].astype(BF16), wb_ref[...])
    merged = jax.nn.sigmoid(g_ref[:, :D_MODEL]) * ya + jax.nn.sigmoid(g_ref[:, D_MODEL:]) * yb
    o_ref[...] = x_ref[...] + _dot(merged.astype(BF16), wo_ref[...])


def _merge(x, oa, ob, gates, wa, wb, wo, *, tm):
    n, d = x.shape
    row = lambda i: (i, 0)
    const = lambda i: (0, 0)
    return pl.pallas_call(
        _merge_body,
        grid=(n // tm,),
        in_specs=[
            pl.BlockSpec((tm, d), row),
            pl.BlockSpec((tm, d), row),
            pl.BlockSpec((tm, d), row),
            pl.BlockSpec((tm, 2 * d), row),
            pl.BlockSpec(wa.shape, const),
            pl.BlockSpec(wb.shape, const),
            pl.BlockSpec(wo.shape, const),
        ],
        out_specs=pl.BlockSpec((tm, d), row),
        out_shape=jax.ShapeDtypeStruct((n, d), F32),
        compiler_params=_params(),
        name="merge",
    )(x, oa, ob, gates, wa, wb, wo)


def _row_tile(n, pref):
    tm = min(pref, n)
    assert n % tm == 0, (n, tm)
    return tm


def kernel(x_prompt, x_sample, cache_a_k, cache_a_v, cache_b_k, cache_b_v, cache_b_kidx, meta, g_ffn1, w1_gate,
           w1_up, w1_down, g_mix, w_in, lam_q1, lam_k1, lam_q2, lam_k2, a_subln, w_a, w_b, w_o, g_ffn2, w2_gate,
           w2_up, w2_down, g_final):
    bsz, seq, d = x_prompt.shape
    dec_b, dec_s, _ = x_sample.shape
    n_cache = cache_a_k.shape[2]
    past = n_cache - N_META
    assert d == D_MODEL and meta.shape == (N_META, D_MODEL)
    assert cache_a_k.shape[0] == 1, "single-layer step"
    assert past % LANES == 0 and (past % CHUNK) + dec_s <= CHUNK, "all cached and new keys visible to every new query"
    n_top_p = min(TOPK_MAX, seq // 4)
    n_top_s = min(TOPK_MAX, (past + dec_s) // 4)

    cast = lambda w: w[0].astype(BF16)
    w1g, w1u, w1d = cast(w1_gate), cast(w1_up), cast(w1_down)
    w2g, w2u, w2d = cast(w2_gate), cast(w2_up), cast(w2_down)
    wa, wb, wo = cast(w_a), cast(w_b), cast(w_o)
    win = w_in[0]
    c0 = 3 * A_WIDTH
    c1 = c0 + 3 * B_WIDTH
    c2 = c1 + IQ_WIDTH
    c3 = c2 + IDX_DIM + IDX_HEADS
    w_pa = win[:, :c0].astype(BF16)
    w_pb = win[:, c0:c1].astype(BF16)
    w_pig = jnp.concatenate(
        [win[:, c1:c3], jnp.zeros((d, LANES - IDX_DIM - IDX_HEADS), F32), win[:, c3:]], axis=1).astype(BF16)
    g1, gm, g2, gf = g_ffn1[0][None], g_mix[0][None], g_ffn2[0][None], g_final[None]
    lam4 = jnp.stack([lam_q1[0], lam_k1[0], lam_q2[0], lam_k2[0]]).astype(F32)
    gsub = a_subln[0][None]

    def front(x, pos, tm_ffn, tm_proj, cache_meta=None):
        x1, h = _ffn(x, g1, w1g, w1u, w1d, gm, tm=tm_ffn, emit_h=True)
        proj, cache_rows = _project_all(h, pos, w_pa, w_pb, w_pig, tm=tm_proj, cache_meta=cache_meta)
        return (x1,) + proj, cache_rows

    def back(x1, oa, ob, gates, tm, tm_ffn):
        x2 = _merge(x1, oa, ob, gates, wa, wb, wo, tm=tm)
        (y,) = _ffn(x2, g2, w2g, w2u, w2d, gf, tm=tm_ffn, emit_h=False)
        return y

    (_, _, ak_m, av_m, _, bk_m, bv_m, _, misc_m, _), _ = front(
        meta, jnp.arange(N_META, dtype=jnp.int32), N_META, N_META)
    pad_meta = lambda a: jnp.pad(a, ((0, META_PAD - N_META), (0, 0)))

    n_p = bsz * seq
    tm_p = _row_tile(seq, 512)
    tm_ffn_p = _row_tile(n_p, 1024)
    pos_p = N_META + jnp.arange(seq, dtype=jnp.int32)
    (x1_p, aq_p, ak_p, av_p, bq_p, bk_p, bv_p, iq_p, misc_p, gates_p), cache_rows_p = front(
        x_prompt.reshape(n_p, d), pos_p, tm_ffn_p, tm_p, cache_meta=(seq, ak_m, av_m, bk_m, bv_m))
    r3 = lambda a: a.reshape(bsz, seq, a.shape[-1])
    tq = _row_tile(seq, 512)
    oa_p = _diff_prompt(r3(aq_p), r3(ak_p), r3(av_p), pad_meta(ak_m), pad_meta(av_m), lam4, gsub, tq=tq)
    bias_m, bias_f = _dsa_index(r3(iq_p), r3(misc_p), pad_meta(misc_m), tq=tq, tk=tq,
                                n_top=n_top_p)
    ob_p = _dsa_attn(r3(bq_p), r3(bk_p), r3(bv_p), pad_meta(bk_m), pad_meta(bv_m), bias_m, bias_f, tq=tq)
    y_p = back(x1_p, oa_p.reshape(n_p, d), ob_p.reshape(n_p, d), gates_p, tm_p, tm_ffn_p).reshape(bsz, seq, d)

    n_s = dec_b * dec_s
    pos_s = N_META + past + jnp.tile(jnp.arange(dec_s, dtype=jnp.int32), dec_b)
    (x1_s, aq_s, ak_s, av_s, bq_s, bk_s, bv_s, iq_s, misc_s, gates_s), _ = front(
        x_sample.reshape(n_s, d), pos_s, n_s, n_s)
    c3d = lambda c: c[0].reshape(dec_b, n_cache * N_HEADS, LANES)
    oa_s = _diff_sample(aq_s, ak_s, av_s, c3d(cache_a_k), c3d(cache_a_v), lam4, gsub, s_new=dec_s)
    ob_s = _dsa_sample(iq_s, misc_s, bq_s, bk_s, bv_s, cache_b_kidx[0], c3d(cache_b_k), c3d(cache_b_v),
                       s_new=dec_s, n_top=n_top_s)
    y_s = back(x1_s, oa_s, ob_s, gates_s, n_s, n_s).reshape(dec_b, dec_s, d)

    ha, hb, hi = (A_HEADS, 2 * A_DIM), (B_HEADS, B_DIM), (IDX_DIM,)
    rows_p = lambda a, heads: a.reshape((1, bsz, N_META + seq) + heads)
    rows_s = lambda a, heads: a.reshape((1, dec_b, dec_s) + heads)
    kidx_p = jnp.concatenate(
        [jnp.broadcast_to(misc_m[None, :, :IDX_DIM], (bsz, N_META, IDX_DIM)), r3(misc_p)[:, :, :IDX_DIM]], axis=1)
    ak_r, av_r, bk_r, bv_r = cache_rows_p
    return (
        y_p, y_s,
        rows_p(ak_r, ha), rows_p(av_r, ha), rows_p(bk_r, hb), rows_p(bv_r, hb), kidx_p[None],
        rows_s(ak_s, ha), rows_s(av_s, ha), rows_s(bk_s, hb), rows_s(bv_s, hb), rows_s(misc_s[:, :IDX_DIM], hi),
    )
```

```python
import functools

import jax
import jax.numpy as jnp
from jax import lax
from jax.experimental import pallas as pl
from jax.experimental.pallas import tpu as pltpu

D_MODEL = 1024
CHUNK = 64
N_META = 16
ROPE_THETA = 10000.0
EPS = 1e-6
A_HEADS = 8
A_DIM = 64
B_HEADS = 8
B_DIM = 128
IDX_HEADS = 8
IDX_DIM = 64
TOPK_MAX = 256
LAM_INIT = 0.2

LANES = 128
N_HEADS = 8
A_WIDTH = A_HEADS * 2 * A_DIM
B_WIDTH = B_HEADS * B_DIM
IQ_WIDTH = IDX_HEADS * IDX_DIM
META_PAD = LANES
SEARCH_ROWS = 128
QBLK = 16

VMEM_LIMIT = 56 * 1024 * 1024
NEG = -1e30
INT_MIN = -(2 ** 31)
KEY_NEG_INF = -2139095041
HI_NEG_INF = KEY_NEG_INF >> 16
I16_MIN = -(2 ** 15)
LOG2E = 1.4426950408889634

F32 = jnp.float32
BF16 = jnp.bfloat16
I16 = jnp.int16

_NT = (((1,), (1,)), ((), ()))


def _params():
    return pltpu.CompilerParams(vmem_limit_bytes=VMEM_LIMIT)


def _dot(a, b):
    return jnp.dot(a, b, preferred_element_type=F32)


def _dot_nt(a, b):
    return lax.dot_general(a, b, _NT, preferred_element_type=F32)


def _rms(x, g):
    ms = jnp.mean(x * x, axis=-1, keepdims=True)
    return x * lax.rsqrt(ms + EPS) * g


def _ffn_body(x_ref, g_ref, wg_ref, wu_ref, wd_ref, gn_ref, *rest, n_ff, emit_h):
    if emit_h:
        o_ref, hn_ref, h_scr, acc_scr = rest
    else:
        o_ref, h_scr, acc_scr = rest
    j = pl.program_id(1)

    @pl.when(j == 0)
    def _():
        h_scr[...] = _rms(x_ref[...], g_ref[...]).astype(BF16)
        acc_scr[...] = jnp.zeros_like(acc_scr)

    h = h_scr[...]
    gate = _dot(h, wg_ref[...])
    up = _dot(h, wu_ref[...])
    act = gate * jax.nn.sigmoid(gate) * up
    acc_scr[...] += _dot(act.astype(BF16), wd_ref[...])

    @pl.when(j == n_ff - 1)
    def _():
        y = x_ref[...] + 0.5 * acc_scr[...]
        if emit_h:
            o_ref[...] = y
            hn_ref[...] = _rms(y, gn_ref[...]).astype(BF16)
        else:
            o_ref[...] = _rms(y, gn_ref[...])


def _ffn(x, g, wg, wu, wd, gn, *, tm, emit_h):
    n, d = x.shape
    dff = wg.shape[1]
    tf = 256 if dff % 256 == 0 else LANES
    n_ff = dff // tf
    row = lambda i, j: (i, 0)
    const = lambda i, j: (0, 0)
    out_shape = [jax.ShapeDtypeStruct((n, d), F32)]
    out_specs = [pl.BlockSpec((tm, d), row)]
    if emit_h:
        out_shape.append(jax.ShapeDtypeStruct((n, d), BF16))
        out_specs.append(pl.BlockSpec((tm, d), row))
    return pl.pallas_call(
        functools.partial(_ffn_body, n_ff=n_ff, emit_h=emit_h),
        grid=(n // tm, n_ff),
        in_specs=[
            pl.BlockSpec((tm, d), row),
            pl.BlockSpec((1, d), const),
            pl.BlockSpec((d, tf), lambda i, j: (0, j)),
            pl.BlockSpec((d, tf), lambda i, j: (0, j)),
            pl.BlockSpec((tf, d), lambda i, j: (j, 0)),
            pl.BlockSpec((1, d), const),
        ],
        out_specs=out_specs,
        out_shape=out_shape,
        scratch_shapes=[pltpu.VMEM((tm, d), BF16), pltpu.VMEM((tm, d), F32)],
        compiler_params=_params(),
        name="ffn_h" if emit_h else "ffn_final",
    )(x, g, wg, wu, wd, gn)


def _rope_block(z, tab_ref, kind, t0):
    cos = tab_ref[:, t0 * LANES:(t0 + 1) * LANES]
    s_a = tab_ref[:, (t0 + 1) * LANES:(t0 + 2) * LANES]
    if kind == "d128":
        return z * cos + pltpu.roll(z, 64, 1) * s_a
    s_b = tab_ref[:, (t0 + 2) * LANES:(t0 + 3) * LANES]
    return z * cos + pltpu.roll(z, 96, 1) * s_a + pltpu.roll(z, 32, 1) * s_b


def _proj_body(h_ref, w_ref, tab_ref, *refs, segs, cache_segs, tm, tiles_per_batch, n_steps):
    n_c = len(cache_segs)
    meta_refs = refs[:n_c]
    out_refs = refs[n_c:n_c + len(segs)]
    cache_refs = refs[n_c + len(segs):2 * n_c + len(segs)]
    if n_c:
        stage, meta_stage, sem, meta_sem = refs[2 * n_c + len(segs):]
    i = pl.program_id(0)
    slot = i % 2
    batch = i // tiles_per_batch
    tile = i % tiles_per_batch

    def frames_copy(c, s):
        first = (N_META + tile * tm) * N_HEADS
        return pltpu.make_async_copy(
            stage.at[c, s], cache_refs[c].at[batch, pl.ds(first, tm * N_HEADS), :], sem.at[c, s])

    if n_c:
        @pl.when(i >= 2)
        def _():
            for c in range(n_c):
                frames_copy(c, slot).wait()

    h = h_ref[...]
    for out_idx, col0, width, rope in segs:
        cw = 256 if width % 256 == 0 else LANES
        for c in range(0, width, cw):
            z = _dot(h, w_ref[:, col0 + c:col0 + c + cw])
            for b in range(cw // LANES):
                zb = z[:, b * LANES:(b + 1) * LANES]
                if rope is not None:
                    zb = _rope_block(zb, tab_ref, rope[0], rope[1])
                out_refs[out_idx][:, c + b * LANES:c + (b + 1) * LANES] = zb.astype(out_refs[out_idx].dtype)
                if out_idx in cache_segs:
                    head = c // LANES + b
                    stage[cache_segs.index(out_idx), slot, pl.ds(head, tm, stride=N_HEADS), :] = zb

    for c in range(n_c):
        frames_copy(c, slot).start()

    if n_c:
        @pl.when(tile == 0)
        def _():
            for c in range(n_c):
                for head in range(N_HEADS):
                    meta_stage[c, pl.ds(head, N_META, stride=N_HEADS), :] = (
                        meta_refs[c][:, head * LANES:(head + 1) * LANES])
                copy = pltpu.make_async_copy(
                    meta_stage.at[c], cache_refs[c].at[batch, pl.ds(0, N_META * N_HEADS), :], meta_sem.at[c])
                copy.start()
                copy.wait()

        @pl.when(i == n_steps - 1)
        def _():
            for c in range(n_c):
                frames_copy(c, slot).wait()
                if n_steps > 1:
                    frames_copy(c, 1 - slot).wait()


def _proj(h, w, tab, segs, *, tm, name, cache=None):
    n, d = h.shape
    widths = [s[2] for s in segs]
    row = lambda i: (i, 0)
    const = lambda i: (0, 0)
    n_steps = n // tm
    seq, meta_rows = cache if cache is not None else (n, {})
    cache_segs = tuple(sorted(meta_rows))
    n_c = len(cache_segs)
    assert seq % tm == 0 and all(widths[s] == N_HEADS * LANES for s in cache_segs)
    assert tab.shape[0] % tm == 0 and n % tab.shape[0] == 0
    tab_tiles = tab.shape[0] // tm
    out_specs = [pl.BlockSpec((tm, wd), row) for wd in widths]
    out_shape = [jax.ShapeDtypeStruct((n, wd), BF16 if s in cache_segs else F32) for s, wd in enumerate(widths)]
    out_specs += [pl.BlockSpec(memory_space=pl.ANY)] * n_c
    out_shape += [jax.ShapeDtypeStruct((n // seq, (N_META + seq) * N_HEADS, LANES), F32)] * n_c
    scratch = []
    if n_c:
        scratch = [pltpu.VMEM((n_c, 2, tm * N_HEADS, LANES), F32), pltpu.VMEM((n_c, N_META * N_HEADS, LANES), F32),
                   pltpu.SemaphoreType.DMA((n_c, 2)), pltpu.SemaphoreType.DMA((n_c,))]
    return pl.pallas_call(
        functools.partial(_proj_body, segs=tuple(segs), cache_segs=cache_segs, tm=tm,
                          tiles_per_batch=seq // tm, n_steps=n_steps),
        grid=(n_steps,),
        in_specs=[
            pl.BlockSpec((tm, d), row),
            pl.BlockSpec(w.shape, const),
            pl.BlockSpec((tm, tab.shape[1]), lambda i: (i % tab_tiles, 0)),
        ] + [pl.BlockSpec((N_META, widths[s]), const) for s in cache_segs],
        out_specs=out_specs,
        out_shape=out_shape,
        scratch_shapes=scratch,
        compiler_params=_params(),
        name=name,
    )(h, w, tab, *[meta_rows[s] for s in cache_segs])


def _rope_tables(pos):
    pos = pos.astype(F32)
    inv32 = ROPE_THETA ** (-jnp.arange(32, dtype=F32) / 32)
    inv64 = ROPE_THETA ** (-jnp.arange(64, dtype=F32) / 64)
    a32 = pos[:, None] * inv32[None, :]
    a64 = pos[:, None] * inv64[None, :]
    c32, s32 = jnp.cos(a32), jnp.sin(a32)
    c64, s64 = jnp.cos(a64), jnp.sin(a64)
    z32 = jnp.zeros_like(c32)
    z64 = jnp.zeros_like(c64)
    return jnp.concatenate(
        [c32, c32, c32, c32, -s32, z32, -s32, z32, z32, s32, z32, s32,
         c64, c64, -s64, s64,
         c32, c32, jnp.ones_like(c64), -s32, z32, z64, z32, s32, z64],
        axis=1)


TAB_D64, TAB_D128, TAB_MISC = 0, 3, 5


def _project_all(h, pos, w_a, w_b, w_ig, *, tm, cache_meta=None):
    tab = _rope_tables(pos)
    cache_a = cache_b = None
    if cache_meta is not None:
        seq, ak_m, av_m, bk_m, bv_m = cache_meta
        cache_a, cache_b = (seq, {1: ak_m, 2: av_m}), (seq, {1: bk_m, 2: bv_m})
    aq, ak, av, *rows_a = _proj(h, w_a, tab,
                                [(0, 0, A_WIDTH, ("d64", TAB_D64)), (1, A_WIDTH, A_WIDTH, ("d64", TAB_D64)),
                                 (2, 2 * A_WIDTH, A_WIDTH, None)], tm=tm, name="proj_a", cache=cache_a)
    bq, bk, bv, *rows_b = _proj(h, w_b, tab,
                                [(0, 0, B_WIDTH, ("d128", TAB_D128)), (1, B_WIDTH, B_WIDTH, ("d128", TAB_D128)),
                                 (2, 2 * B_WIDTH, B_WIDTH, None)], tm=tm, name="proj_b", cache=cache_b)
    iq, misc, gates = _proj(h, w_ig, tab,
                            [(0, 0, IQ_WIDTH, ("d64", TAB_D64)), (1, IQ_WIDTH, LANES, ("d64", TAB_MISC)),
                             (2, IQ_WIDTH + LANES, 2 * D_MODEL, None)], tm=tm, name="proj_ig")
    return (aq, ak, av, bq, bk, bv, iq, misc, gates), tuple(rows_a + rows_b)


def _lam(lam_ref):
    lp = lam_ref[...]
    s1 = jnp.sum(lp[0:1] * lp[1:2], axis=-1, keepdims=True)
    s2 = jnp.sum(lp[2:3] * lp[3:4], axis=-1, keepdims=True)
    return jnp.exp(s1) - jnp.exp(s2) + LAM_INIT


def _stack_diff_queries(q):
    lane = lax.broadcasted_iota(jnp.int32, q.shape, 1)
    q = q * (A_DIM ** -0.5 * LOG2E)
    q1 = jnp.where(lane < A_DIM, q, 0.0)
    q2 = jnp.where(lane >= A_DIM, q, 0.0)
    return jnp.concatenate([q1, q2], axis=0).astype(BF16)


def _diff_finish(acc, l, n, lam, g):
    o1 = acc[:n] * (1.0 / l[:n])
    o2 = acc[n:] * (1.0 / l[n:])
    o = o1 - lam * o2
    return _rms(o, g) * (1.0 - LAM_INIT)


def _diff_prompt_body(lam_ref, q_ref, k_ref, v_ref, km_ref, vm_ref, g_ref, o_ref, *, tq, tk, heads):
    qi = pl.program_id(2)
    lanes = [slice(u * LANES, (u + 1) * LANES) for u in range(heads)]
    qs = [_stack_diff_queries(q_ref[0, :, ln]) for ln in lanes]

    def update_head(q, state, k, v, visible):
        m, l, acc = state
        s = _dot_nt(q, k)
        if visible is not None:
            s = jnp.where(visible, s, NEG)
        m_new = jnp.maximum(m, jnp.max(s, axis=-1, keepdims=True))
        alpha = jnp.exp2(m - m_new)
        p = jnp.exp2(s - m_new)
        return m_new, alpha * l + jnp.sum(p, axis=-1, keepdims=True), alpha * acc + _dot(p.astype(BF16), v)

    def update(states, k, v, visible):
        return tuple(update_head(qs[u], states[u], k[:, lanes[u]].astype(BF16), v[:, lanes[u]].astype(BF16), visible)
                     for u in range(heads))

    def load(start, n):
        return k_ref[0, pl.ds(start, n), :], v_ref[0, pl.ds(start, n), :]

    col = lax.broadcasted_iota(jnp.int32, (2 * tq, META_PAD), 1)
    init = (jnp.full((2 * tq, 1), NEG, F32), jnp.zeros((2 * tq, 1), F32), jnp.zeros((2 * tq, LANES), F32))
    states = update((init,) * heads, km_ref[...], vm_ref[...], col < N_META)

    n_wide = (qi * tq) // tk
    states = lax.fori_loop(
        0, n_wide, lambda j, st: update(st, *load(pl.multiple_of(j * tk, tk), tk), None), states)
    states = lax.fori_loop(
        n_wide * (tk // tq), qi, lambda j, st: update(st, *load(pl.multiple_of(j * tq, tq), tq), None), states)
    r = lax.broadcasted_iota(jnp.int32, (2 * tq, tq), 0)
    c = lax.broadcasted_iota(jnp.int32, (2 * tq, tq), 1)
    r = jnp.where(r >= tq, r - tq, r)
    states = update(states, *load(pl.multiple_of(qi * tq, tq), tq), c // CHUNK <= r // CHUNK)
    lam = _lam(lam_ref)
    for u in range(heads):
        m, l, acc = states[u]
        o_ref[0, :, lanes[u]] = _diff_finish(acc, l, tq, lam, g_ref[...])


def _diff_prompt(aq, ak, av, akm, avm, lam4, g, *, tq):
    bsz, t, _ = aq.shape
    heads = 2
    tk = min(4 * tq, t)
    width = heads * LANES
    head_q = lambda b, h, i: (b, i, h)
    head_all = lambda b, h, i: (b, 0, h)
    head_meta = lambda b, h, i: (0, h)
    const = lambda b, h, i: (0, 0)
    return pl.pallas_call(
        functools.partial(_diff_prompt_body, tq=tq, tk=tk, heads=heads),
        grid=(bsz, N_HEADS // heads, t // tq),
        in_specs=[
            pl.BlockSpec(lam4.shape, const),
            pl.BlockSpec((1, tq, width), head_q),
            pl.BlockSpec((1, t, width), head_all),
            pl.BlockSpec((1, t, width), head_all),
            pl.BlockSpec((META_PAD, width), head_meta),
            pl.BlockSpec((META_PAD, width), head_meta),
            pl.BlockSpec((1, LANES), const),
        ],
        out_specs=pl.BlockSpec((1, tq, width), head_q),
        out_shape=jax.ShapeDtypeStruct(aq.shape, F32),
        compiler_params=_params(),
        name="diff_attn_prompt",
    )(lam4, aq, ak, av, akm, avm, g)


def _cache_head(cache_ref, h, first, n):
    return cache_ref[0, pl.ds(first * N_HEADS + h, n, stride=N_HEADS), :]


def _small_keys(cache_ref, new_ref, h, s_new):
    meta = _cache_head(cache_ref, h, 0, N_META)
    new = new_ref[:, h * LANES:(h + 1) * LANES]
    pad = jnp.zeros((META_PAD - N_META - s_new, LANES), F32)
    return jnp.concatenate([meta, new, pad], axis=0)


def _diff_sample_body(lam_ref, q_ref, kc_ref, vc_ref, kn_ref, vn_ref, g_ref, o_ref, *, s_new, past):
    lam = _lam(lam_ref)
    for h in range(N_HEADS):
        sl = slice(h * LANES, (h + 1) * LANES)
        qs = _stack_diff_queries(q_ref[:, sl])
        kp = _cache_head(kc_ref, h, N_META, past).astype(BF16)
        vp = _cache_head(vc_ref, h, N_META, past).astype(BF16)
        ks = _small_keys(kc_ref, kn_ref, h, s_new).astype(BF16)
        vs = _small_keys(vc_ref, vn_ref, h, s_new).astype(BF16)
        s_p = _dot_nt(qs, kp)
        s_s = _dot_nt(qs, ks)
        col = lax.broadcasted_iota(jnp.int32, s_s.shape, 1)
        s_s = jnp.where(col < N_META + s_new, s_s, NEG)
        m = jnp.maximum(jnp.max(s_p, axis=-1, keepdims=True), jnp.max(s_s, axis=-1, keepdims=True))
        p_p = jnp.exp2(s_p - m)
        p_s = jnp.exp2(s_s - m)
        l = jnp.sum(p_p, axis=-1, keepdims=True) + jnp.sum(p_s, axis=-1, keepdims=True)
        acc = _dot(p_p.astype(BF16), vp) + _dot(p_s.astype(BF16), vs)
        o_ref[:, sl] = _diff_finish(acc, l, s_new, lam, g_ref[...])


def _diff_sample(aq, ak, av, cache_k, cache_v, lam4, g, *, s_new):
    bsz, n_rows, _ = cache_k.shape
    width = aq.shape[1]
    past = n_rows // N_HEADS - N_META
    rows = lambda b: (b, 0)
    cache = lambda b: (b, 0, 0)
    const = lambda b: (0, 0)
    return pl.pallas_call(
        functools.partial(_diff_sample_body, s_new=s_new, past=past),
        grid=(bsz,),
        in_specs=[
            pl.BlockSpec(lam4.shape, const),
            pl.BlockSpec((s_new, width), rows),
            pl.BlockSpec((1, n_rows, LANES), cache),
            pl.BlockSpec((1, n_rows, LANES), cache),
            pl.BlockSpec((s_new, width), rows),
            pl.BlockSpec((s_new, width), rows),
            pl.BlockSpec((1, LANES), const),
        ],
        out_specs=pl.BlockSpec((s_new, width), rows),
        out_shape=jax.ShapeDtypeStruct(aq.shape, F32),
        compiler_params=_params(),
        name="diff_attn_sample",
    )(lam4, aq, cache_k, cache_v, ak, av, g)


def _stack_index_queries(iq, width):
    n = iq.shape[0]
    lane = lax.broadcasted_iota(jnp.int32, (n, LANES), 1)
    parts = []
    for h in range(IDX_HEADS):
        blk = iq[:, (h // 2) * LANES:(h // 2 + 1) * LANES]
        if h % 2:
            blk = pltpu.roll(blk, IDX_DIM, 1)
        blk = jnp.where(lane < IDX_DIM, blk, 0.0) * (IDX_DIM ** -0.5)
        parts.append(blk[:, :width])
    return jnp.concatenate(parts, axis=0).astype(BF16)


def _index_scores(qs, w, keys, n):
    rel = jnp.maximum(_dot_nt(qs, keys.astype(BF16)), 0.0)
    sc = rel[0:n] * w[:, IDX_DIM:IDX_DIM + 1]
    for h in range(1, IDX_HEADS):
        sc = sc + rel[h * n:(h + 1) * n] * w[:, IDX_DIM + h:IDX_DIM + h + 1]
    return sc * (IDX_HEADS ** -0.5)


def _block_order(qs, n):
    return jnp.concatenate([qs[h * n + b * QBLK:h * n + (b + 1) * QBLK]
                            for b in range(n // QBLK) for h in range(IDX_HEADS)], axis=0)


def _index_scores_blocked(qs, w, keys, n):
    rel = jnp.maximum(_dot_nt(qs, keys.astype(BF16)), 0.0)
    blocks = []
    for b in range(n // QBLK):
        wb = w[b * QBLK:(b + 1) * QBLK]
        sc = rel[b * IDX_HEADS * QBLK:(b * IDX_HEADS + 1) * QBLK] * wb[:, IDX_DIM:IDX_DIM + 1]
        for h in range(1, IDX_HEADS):
            r0 = (b * IDX_HEADS + h) * QBLK
            sc = sc + rel[r0:r0 + QBLK] * wb[:, IDX_DIM + h:IDX_DIM + h + 1]
        blocks.append(sc)
    return jnp.concatenate(blocks, axis=0) * (IDX_HEADS ** -0.5)


def _sort_key(x):
    bits = lax.bitcast_convert_type(x, jnp.int32)
    key = bits ^ ((bits >> 31) & 0x7FFFFFFF)
    return jnp.where(key == -1, 0, key)


def _ind(pred):
    return jnp.where(pred, 1.0, 0.0)


def _lane_sum(x):
    acc = x[:, 0:LANES]
    for b in range(1, x.shape[1] // LANES):
        acc = acc + x[:, b * LANES:(b + 1) * LANES]
    return acc


def _select_threshold(count_fn, n_rows, n_top, idx_bits):
    def bit_step(i, thr):
        cand = thr ^ jnp.left_shift(jnp.int32(1), 31 - i)
        cnt = count_fn(lambda key, col, rows: _ind(key >= cand[rows]))
        return jnp.where(cnt >= n_top, cand, thr)

    thr = lax.fori_loop(0, 32, bit_step, jnp.full((n_rows, 1), INT_MIN, jnp.int32))
    need = n_top - count_fn(lambda key, col, rows: _ind(key > thr[rows]))
    excess = count_fn(lambda key, col, rows: _ind(key == thr[rows])) - need

    def tie_search(_):
        def idx_step(i, cut):
            cand = cut | jnp.left_shift(jnp.int32(1), idx_bits - 1 - i)
            cnt = count_fn(lambda key, col, rows: jnp.where(key == thr[rows], _ind(col < cand[rows]), 0.0))
            return jnp.where(cnt < need, cand, cut)
        return lax.fori_loop(0, idx_bits, idx_step, jnp.zeros((n_rows, 1), jnp.int32))

    no_ties = lambda _: jnp.full((n_rows, 1), 2 ** 30, jnp.int32)
    cut = lax.cond(jnp.max(excess) > 0.0, tie_search, no_ties, None)
    return thr, cut


def _select_bias(key, col, thr, cut):
    tie = jnp.where(key == thr, jnp.where(col <= cut, 0.0, NEG), NEG)
    bias = jnp.where(key > thr, 0.0, tie)
    return jnp.where(key > KEY_NEG_INF, bias, NEG)


def _split_key(key):
    hi = (key >> 16).astype(I16)
    lo = ((key & 0xFFFF) - 32768).astype(I16)
    return hi, lo


def _ind16(pred):
    return jnp.where(pred, jnp.ones(pred.shape, I16), jnp.zeros(pred.shape, I16))


def _dsa_index_body(iq_ref, wq_ref, ik_ref, ikm_ref, om_ref, of_ref, hi_m, lo_m, hi_f, lo_f, *, tq, tk,
                    n_tiles_all, n_top):
    qi = pl.program_id(1)
    qs = _block_order(_stack_index_queries(iq_ref[0], LANES), tq)
    w = wq_ref[0]

    sc = _index_scores_blocked(qs, w, ikm_ref[...], tq)
    lane = lax.broadcasted_iota(jnp.int32, (tq, LANES), 1)
    hi_m[...], lo_m[...] = _split_key(_sort_key(jnp.where(lane < N_META, sc, -jnp.inf)))

    row0 = qi * tq
    n_tiles = (row0 + tq - 1) // tk + 1
    rowc = (row0 + lax.broadcasted_iota(jnp.int32, (tq, tk), 0)) // CHUNK
    colt = lax.broadcasted_iota(jnp.int32, (tq, tk), 1)

    def score_tile(j, masked):
        start = pl.multiple_of(j * tk, tk)
        sc = _index_scores_blocked(qs, w, ik_ref[0, pl.ds(start, tk), :], tq)
        if masked:
            sc = jnp.where((start + colt) // CHUNK <= rowc, sc, -jnp.inf)
        hi_f[j], lo_f[j] = _split_key(_sort_key(sc))
        return 0

    n_open = row0 // tk
    lax.fori_loop(0, n_open, lambda j, _: score_tile(j, False), 0)
    lax.fori_loop(n_open, n_tiles, lambda j, _: score_tile(j, True), 0)

    def count(f):
        accs = []
        for r0 in range(0, tq, SEARCH_ROWS):
            rows = slice(r0, r0 + SEARCH_ROWS)
            accs.append(lax.fori_loop(
                0, n_tiles, lambda j, a: a + _lane_sum(f(hi_f[j, rows, :], lo_f[j, rows, :], rows)),
                f(hi_m[rows, :], lo_m[rows, :], rows)))
        return jnp.sum(jnp.concatenate(accs, axis=0).astype(F32), axis=-1, keepdims=True)

    def half_search(f_ge, need):
        def step(i, t):
            cand = t + jnp.left_shift(jnp.int32(1), 15 - i)
            c16 = cand.astype(I16)
            return jnp.where(count(f_ge(c16)) >= need, cand, t)
        return lax.fori_loop(0, 16, step, jnp.full((tq, 1), I16_MIN, jnp.int32)).astype(I16)

    th = half_search(lambda c: lambda hi, lo, rows: _ind16(hi >= c[rows]), n_top)
    need_lo = n_top - count(lambda hi, lo, rows: _ind16(hi > th[rows]))

    lo_m[...] = jnp.where(hi_m[...] == th, lo_m[...], I16_MIN)

    def tie_class_tile(j, _):
        lo_f[j] = jnp.where(hi_f[j] == th, lo_f[j], I16_MIN)
        return 0

    lax.fori_loop(0, n_tiles, tie_class_tile, 0)
    tl = half_search(lambda c: lambda hi, lo, rows: _ind16(lo >= c[rows]), need_lo)
    need = need_lo - count(lambda hi, lo, rows: _ind16(lo > tl[rows]))

    def bias(hi, lo, tri, before):
        one = jnp.ones(hi.shape, BF16)
        zero = jnp.zeros(hi.shape, BF16)
        above = jnp.where(hi > th, one, jnp.where(hi == th, jnp.where(lo > tl, one, zero), zero))
        tie = jnp.where(hi == th, jnp.where(lo == tl, one, zero), zero)
        tie = jnp.where(hi > HI_NEG_INF, tie, zero)
        rank = _dot(tie, tri)
        take = jnp.where(rank + before <= need, 1.0, 0.0).astype(BF16)
        chosen = above + tie * take
        return jnp.where(chosen > 0.5, zero, jnp.full(hi.shape, NEG, BF16)), before + rank[:, -1:]

    def upper_ones(n):
        r = lax.broadcasted_iota(jnp.int32, (n, n), 0)
        c = lax.broadcasted_iota(jnp.int32, (n, n), 1)
        return jnp.where(r <= c, 1.0, 0.0).astype(BF16)

    om_ref[0], ties = bias(hi_m[...], lo_m[...], upper_ones(META_PAD), jnp.zeros((tq, 1), F32))
    tri = upper_ones(tk)

    def write_tile(j, before):
        of_ref[0, j], after = bias(hi_f[j], lo_f[j], tri, before)
        return after

    lax.fori_loop(0, n_tiles, write_tile, ties)

    def fill_tile(j, _):
        of_ref[0, j] = jnp.full((tq, tk), NEG, BF16)
        return 0

    lax.fori_loop(n_tiles, n_tiles_all, fill_tile, 0)


def _dsa_index(iq, misc, misc_meta, *, tq, tk, n_top):
    bsz, t, _ = iq.shape
    nk = t // tk
    return pl.pallas_call(
        functools.partial(_dsa_index_body, tq=tq, tk=tk, n_tiles_all=nk, n_top=n_top),
        grid=(bsz, t // tq),
        in_specs=[
            pl.BlockSpec((1, tq, IQ_WIDTH), lambda b, i: (b, i, 0)),
            pl.BlockSpec((1, tq, LANES), lambda b, i: (b, i, 0)),
            pl.BlockSpec((1, t, LANES), lambda b, i: (b, 0, 0)),
            pl.BlockSpec((META_PAD, LANES), lambda b, i: (0, 0)),
        ],
        out_specs=[
            pl.BlockSpec((1, tq, LANES), lambda b, i: (b, i, 0)),
            pl.BlockSpec((1, nk, tq, tk), lambda b, i: (b, 0, i, 0)),
        ],
        out_shape=[
            jax.ShapeDtypeStruct((bsz, t, LANES), BF16),
            jax.ShapeDtypeStruct((bsz, nk, t, tk), BF16),
        ],
        scratch_shapes=[pltpu.VMEM((tq, LANES), I16), pltpu.VMEM((tq, LANES), I16),
                        pltpu.VMEM((nk, tq, tk), I16), pltpu.VMEM((nk, tq, tk), I16)],
        compiler_params=_params(),
        name="dsa_index",
    )(iq, misc, misc, misc_meta)


def _dsa_query(q):
    return (q * (B_DIM ** -0.5 * LOG2E)).astype(BF16)


def _masked_flash_update(q, k, v, bias, state):
    m, l, acc = state
    s = _dot_nt(q, k) + bias
    m_new = jnp.maximum(m, jnp.max(s, axis=-1, keepdims=True))
    alpha = jnp.exp2(m - m_new)
    p = jnp.exp2(s - m_new)
    return m_new, alpha * l + jnp.sum(p, axis=-1, keepdims=True), alpha * acc + _dot(p.astype(BF16), v)


def _flash_init(n):
    return jnp.full((n, 1), NEG, F32), jnp.zeros((n, 1), F32), jnp.zeros((n, LANES), F32)


def _dsa_attn_body(q_ref, k_ref, v_ref, km_ref, vm_ref, bm_ref, bf_ref, o_ref, *, tq, tb, group, heads):
    qi = pl.program_id(2)
    lanes = [slice(u * LANES, (u + 1) * LANES) for u in range(heads)]
    qs = [_dsa_query(q_ref[0, :, ln]) for ln in lanes]

    def update(states, k, v, bias):
        return tuple(_masked_flash_update(qs[u], k[:, lanes[u]].astype(BF16), v[:, lanes[u]].astype(BF16), bias,
                                          states[u]) for u in range(heads))

    def tiles(j0, n):
        start = pl.multiple_of(j0 * tb, tb)
        bias = jnp.concatenate([bf_ref[0, j0 + u].astype(F32) for u in range(n)], axis=1)
        return k_ref[0, pl.ds(start, n * tb), :], v_ref[0, pl.ds(start, n * tb), :], bias

    states = update((_flash_init(tq),) * heads, km_ref[...], vm_ref[...], bm_ref[0].astype(F32))
    n_tiles = (qi * tq + tq - 1) // tb + 1
    n_wide = n_tiles // group
    states = lax.fori_loop(0, n_wide, lambda g, st: update(st, *tiles(g * group, group)), states)
    states = lax.fori_loop(n_wide * group, n_tiles, lambda j, st: update(st, *tiles(j, 1)), states)
    for u in range(heads):
        m, l, acc = states[u]
        o_ref[0, :, lanes[u]] = acc * (1.0 / l)


def _dsa_attn(bq, bk, bv, bkm, bvm, bias_m, bias_f, *, tq):
    bsz, t, _ = bq.shape
    nb, tb = bias_f.shape[1], bias_f.shape[3]
    heads = 2
    group = max(1, min(2048, t) // tb)
    width = heads * LANES
    head_q = lambda b, h, i: (b, i, h)
    head_all = lambda b, h, i: (b, 0, h)
    head_meta = lambda b, h, i: (0, h)
    return pl.pallas_call(
        functools.partial(_dsa_attn_body, tq=tq, tb=tb, group=group, heads=heads),
        grid=(bsz, N_HEADS // heads, t // tq),
        in_specs=[
            pl.BlockSpec((1, tq, width), head_q),
            pl.BlockSpec((1, t, width), head_all),
            pl.BlockSpec((1, t, width), head_all),
            pl.BlockSpec((META_PAD, width), head_meta),
            pl.BlockSpec((META_PAD, width), head_meta),
            pl.BlockSpec((1, tq, LANES), lambda b, h, i: (b, i, 0)),
            pl.BlockSpec((1, nb, tq, tb), lambda b, h, i: (b, 0, i, 0)),
        ],
        out_specs=pl.BlockSpec((1, tq, width), head_q),
        out_shape=jax.ShapeDtypeStruct(bq.shape, F32),
        compiler_params=_params(),
        name="dsa_attn_prompt",
    )(bq, bk, bv, bkm, bvm, bias_m, bias_f)


def _dsa_sample_body(iq_ref, wq_ref, kidx_ref, q_ref, kc_ref, vc_ref, kn_ref, vn_ref, o_ref, *,
                     s_new, past, n_top, idx_bits):
    qs = _stack_index_queries(iq_ref[...], IDX_DIM)
    w = wq_ref[...]
    ik_past = kidx_ref[0, N_META:N_META + past, :]
    ik_small = jnp.concatenate(
        [kidx_ref[0, 0:N_META, :], w[:, 0:IDX_DIM], jnp.zeros((META_PAD - N_META - s_new, IDX_DIM), F32)], axis=0)
    lane_s = lax.broadcasted_iota(jnp.int32, (s_new, META_PAD), 1)
    sc_s = _index_scores(qs, w, ik_small, s_new)
    key_s = _sort_key(jnp.where(lane_s < N_META + s_new, sc_s, -jnp.inf))
    key_p = _sort_key(_index_scores(qs, w, ik_past, s_new))
    col_s = jnp.where(lane_s < N_META, lane_s, lane_s + past)
    col_p = N_META + lax.broadcasted_iota(jnp.int32, (s_new, past), 1)

    every = slice(None)
    count_fn = lambda f: jnp.sum(
        f(key_s, col_s, every) + _lane_sum(f(key_p, col_p, every)), axis=-1, keepdims=True)
    thr, cut = _select_threshold(count_fn, s_new, n_top, idx_bits)
    bias_s = _select_bias(key_s, col_s, thr, cut)
    bias_p = _select_bias(key_p, col_p, thr, cut)

    for h in range(N_HEADS):
        sl = slice(h * LANES, (h + 1) * LANES)
        q = _dsa_query(q_ref[:, sl])
        state = _masked_flash_update(q, _small_keys(kc_ref, kn_ref, h, s_new).astype(BF16),
                                     _small_keys(vc_ref, vn_ref, h, s_new).astype(BF16), bias_s,
                                     _flash_init(s_new))
        m, l, acc = _masked_flash_update(q, _cache_head(kc_ref, h, N_META, past).astype(BF16),
                                         _cache_head(vc_ref, h, N_META, past).astype(BF16), bias_p, state)
        o_ref[:, sl] = acc * (1.0 / l)


def _dsa_sample(iq, misc, bq, bk, bv, cache_kidx, cache_k, cache_v, *, s_new, n_top):
    bsz, n_rows, _ = cache_k.shape
    width = bq.shape[1]
    n_cache = n_rows // N_HEADS
    past = n_cache - N_META
    idx_bits = (n_cache + s_new - 1).bit_length()
    rows = lambda b: (b, 0)
    cache = lambda b: (b, 0, 0)
    return pl.pallas_call(
        functools.partial(_dsa_sample_body, s_new=s_new, past=past, n_top=n_top, idx_bits=idx_bits),
        grid=(bsz,),
        in_specs=[
            pl.BlockSpec((s_new, IQ_WIDTH), rows),
            pl.BlockSpec((s_new, LANES), rows),
            pl.BlockSpec((1, n_cache, IDX_DIM), cache),
            pl.BlockSpec((s_new, width), rows),
            pl.BlockSpec((1, n_rows, LANES), cache),
            pl.BlockSpec((1, n_rows, LANES), cache),
            pl.BlockSpec((s_new, width), rows),
            pl.BlockSpec((s_new, width), rows),
        ],
        out_specs=pl.BlockSpec((s_new, width), rows),
        out_shape=jax.ShapeDtypeStruct(bq.shape, F32),
        compiler_params=_params(),
        name="dsa_sample",
    )(iq, misc, cache_kidx, bq, cache_k, cache_v, bk, bv)


def _merge_body(x_ref, oa_ref, ob_ref, g_ref, wa_ref, wb_ref, wo_ref, o_ref):
    ya = _dot(oa_ref[...].astype(BF16), wa_ref[...])
    yb = _dot(ob_ref[...].astype(BF16), wb_ref[...])
    merged = jax.nn.sigmoid(g_ref[:, :D_MODEL]) * ya + jax.nn.sigmoid(g_ref[:, D_MODEL:]) * yb
    o_ref[...] = x_ref[...] + _dot(merged.astype(BF16), wo_ref[...])


def _merge(x, oa, ob, gates, wa, wb, wo, *, tm):
    n, d = x.shape
    row = lambda i: (i, 0)
    const = lambda i: (0, 0)
    return pl.pallas_call(
        _merge_body,
        grid=(n // tm,),
        in_specs=[
            pl.BlockSpec((tm, d), row),
            pl.BlockSpec((tm, d), row),
            pl.BlockSpec((tm, d), row),
            pl.BlockSpec((tm, 2 * d), row),
            pl.BlockSpec(wa.shape, const),
            pl.BlockSpec(wb.shape, const),
            pl.BlockSpec(wo.shape, const),
        ],
        out_specs=pl.BlockSpec((tm, d), row),
        out_shape=jax.ShapeDtypeStruct((n, d), F32),
        compiler_params=_params(),
        name="merge",
    )(x, oa, ob, gates, wa, wb, wo)


def _row_tile(n, pref):
    tm = min(pref, n)
    assert n % tm == 0, (n, tm)
    return tm


def kernel(x_prompt, x_sample, cache_a_k, cache_a_v, cache_b_k, cache_b_v, cache_b_kidx, meta, g_ffn1, w1_gate,
           w1_up, w1_down, g_mix, w_in, lam_q1, lam_k1, lam_q2, lam_k2, a_subln, w_a, w_b, w_o, g_ffn2, w2_gate,
           w2_up, w2_down, g_final):
    bsz, seq, d = x_prompt.shape
    dec_b, dec_s, _ = x_sample.shape
    n_cache = cache_a_k.shape[2]
    past = n_cache - N_META
    assert d == D_MODEL and meta.shape == (N_META, D_MODEL)
    assert cache_a_k.shape[0] == 1, "single-layer step"
    assert past % LANES == 0 and (past % CHUNK) + dec_s <= CHUNK, "all cached and new keys visible to every new query"
    n_top_p = min(TOPK_MAX, seq // 4)
    n_top_s = min(TOPK_MAX, (past + dec_s) // 4)

    cast = lambda w: w[0].astype(BF16)
    w1g, w1u, w1d = cast(w1_gate), cast(w1_up), cast(w1_down)
    w2g, w2u, w2d = cast(w2_gate), cast(w2_up), cast(w2_down)
    wa, wb, wo = cast(w_a), cast(w_b), cast(w_o)
    win = w_in[0]
    c0 = 3 * A_WIDTH
    c1 = c0 + 3 * B_WIDTH
    c2 = c1 + IQ_WIDTH
    c3 = c2 + IDX_DIM + IDX_HEADS
    w_pa = win[:, :c0].astype(BF16)
    w_pb = win[:, c0:c1].astype(BF16)
    w_pig = jnp.concatenate(
        [win[:, c1:c3], jnp.zeros((d, LANES - IDX_DIM - IDX_HEADS), F32), win[:, c3:]], axis=1).astype(BF16)
    g1, gm, g2, gf = g_ffn1[0][None], g_mix[0][None], g_ffn2[0][None], g_final[None]
    lam4 = jnp.stack([lam_q1[0], lam_k1[0], lam_q2[0], lam_k2[0]]).astype(F32)
    gsub = a_subln[0][None]

    def front(x, pos, tm_ffn, tm_proj, cache_meta=None):
        x1, h = _ffn(x, g1, w1g, w1u, w1d, gm, tm=tm_ffn, emit_h=True)
        proj, cache_rows = _project_all(h, pos, w_pa, w_pb, w_pig, tm=tm_proj, cache_meta=cache_meta)
        return (x1,) + proj, cache_rows

    def back(x1, oa, ob, gates, tm, tm_ffn):
        x2 = _merge(x1, oa, ob, gates, wa, wb, wo, tm=tm)
        (y,) = _ffn(x2, g2, w2g, w2u, w2d, gf, tm=tm_ffn, emit_h=False)
        return y

    (_, _, ak_m, av_m, _, bk_m, bv_m, _, misc_m, _), _ = front(
        meta, jnp.arange(N_META, dtype=jnp.int32), N_META, N_META)
    pad_meta = lambda a: jnp.pad(a, ((0, META_PAD - N_META), (0, 0)))

    n_p = bsz * seq
    tm_p = _row_tile(seq, 512)
    tm_ffn_p = _row_tile(n_p, 1024)
    pos_p = N_META + jnp.arange(seq, dtype=jnp.int32)
    (x1_p, aq_p, ak_p, av_p, bq_p, bk_p, bv_p, iq_p, misc_p, gates_p), cache_rows_p = front(
        x_prompt.reshape(n_p, d), pos_p, tm_ffn_p, tm_p, cache_meta=(seq, ak_m, av_m, bk_m, bv_m))
    r3 = lambda a: a.reshape(bsz, seq, a.shape[-1])
    tq = _row_tile(seq, 512)
    oa_p = _diff_prompt(r3(aq_p), r3(ak_p), r3(av_p), pad_meta(ak_m), pad_meta(av_m), lam4, gsub, tq=tq)
    bias_m, bias_f = _dsa_index(r3(iq_p), r3(misc_p), pad_meta(misc_m), tq=tq, tk=tq,
                                n_top=n_top_p)
    ob_p = _dsa_attn(r3(bq_p), r3(bk_p), r3(bv_p), pad_meta(bk_m), pad_meta(bv_m), bias_m, bias_f, tq=tq)
    y_p = back(x1_p, oa_p.reshape(n_p, d), ob_p.reshape(n_p, d), gates_p, tm_p, tm_ffn_p).reshape(bsz, seq, d)

    n_s = dec_b * dec_s
    pos_s = N_META + past + jnp.tile(jnp.arange(dec_s, dtype=jnp.int32), dec_b)
    (x1_s, aq_s, ak_s, av_s, bq_s, bk_s, bv_s, iq_s, misc_s, gates_s), _ = front(
        x_sample.reshape(n_s, d), pos_s, n_s, n_s)
    c3d = lambda c: c[0].reshape(dec_b, n_cache * N_HEADS, LANES)
    oa_s = _diff_sample(aq_s, ak_s, av_s, c3d(cache_a_k), c3d(cache_a_v), lam4, gsub, s_new=dec_s)
    ob_s = _dsa_sample(iq_s, misc_s, bq_s, bk_s, bv_s, cache_b_kidx[0], c3d(cache_b_k), c3d(cache_b_v),
                       s_new=dec_s, n_top=n_top_s)
    y_s = back(x1_s, oa_s, ob_s, gates_s, n_s, n_s).reshape(dec_b, dec_s, d)

    ha, hb, hi = (A_HEADS, 2 * A_DIM), (B_HEADS, B_DIM), (IDX_DIM,)
    rows_p = lambda a, heads: a.reshape((1, bsz, N_META + seq) + heads)
    rows_s = lambda a, heads: a.reshape((1, dec_b, dec_s) + heads)
    kidx_p = jnp.concatenate(
        [jnp.broadcast_to(misc_m[None, :, :IDX_DIM], (bsz, N_META, IDX_DIM)), r3(misc_p)[:, :, :IDX_DIM]], axis=1)
    ak_r, av_r, bk_r, bv_r = cache_rows_p
    return (
        y_p, y_s,
        rows_p(ak_r, ha), rows_p(av_r, ha), rows_p(bk_r, hb), rows_p(bv_r, hb), kidx_p[None],
        rows_s(ak_s, ha), rows_s(av_s, ha), rows_s(bk_s, hb), rows_s(bv_s, hb), rows_s(misc_s[:, :IDX_DIM], hi),
    )
```
